```python
import math
import jax, jax.numpy as jnp
from jax import lax
import numpy as np

D_MODEL = 2048
BATCH = 8
SEQ = 4096
DEPTH = 2

N_META = 16
MLA_HEADS = 16
MLA_NOPE_DIM = 128
MLA_ROPE_DIM = 64
MLA_V_DIM = 128
MLA_Q_RANK = 768
MLA_KV_RANK = 512
SWA_HEADS = 16
SWA_KV_HEADS = 4
SWA_GROUP = SWA_HEADS // SWA_KV_HEADS
SWA_HEAD_DIM = 128
WINDOW = 128
BLOCK = 128
N_BUCKETS = 32
MAX_DISTANCE = 128
D_FF = 5632
N_EXPERTS = 8
TOP_K = 2
MOE_BLOCK = 512
N_DENSE = (DEPTH + 1) // 2
N_MOE = DEPTH // 2
ROPE_THETA = 10000.0
LN_EPS = 1e-5
RMS_EPS = 1e-6
ALPHA = (2 * DEPTH) ** 0.25
BETA = (8 * DEPTH) ** -0.25
NEG = -1e30
SPLITS = (MLA_Q_RANK, MLA_KV_RANK, MLA_ROPE_DIM, SWA_HEADS * SWA_HEAD_DIM,
          SWA_KV_HEADS * SWA_HEAD_DIM, SWA_KV_HEADS * SWA_HEAD_DIM, D_MODEL, D_MODEL)
IN_DIM = sum(SPLITS)

kernel_name = 'hybrid_mla_swa_gated_moe_encoder'


def layer_norm(x, g, b):
    xf = x.astype(jnp.float32)
    mu = xf.mean(-1, keepdims=True)
    var = jnp.square(xf - mu).mean(-1, keepdims=True)
    y = (xf - mu) * lax.rsqrt(var + LN_EPS) * g.astype(jnp.float32) + b.astype(jnp.float32)
    return y.astype(x.dtype)


def rms_norm(x, g):
    xf = x.astype(jnp.float32)
    y = xf * lax.rsqrt(jnp.square(xf).mean(-1, keepdims=True) + RMS_EPS) * g.astype(jnp.float32)
    return y.astype(x.dtype)


def rope_tables(n):
    pos = jnp.arange(n, dtype=jnp.float32)
    inv = ROPE_THETA ** (-jnp.arange(0, MLA_ROPE_DIM, 2, dtype=jnp.float32) / MLA_ROPE_DIM)
    ang = pos[:, None] * inv[None, :]
    return jnp.cos(ang), jnp.sin(ang)


def apply_rope(x, cos, sin):
    half = x.shape[-1] // 2
    x1 = x[..., :half].astype(jnp.float32)
    x2 = x[..., half:].astype(jnp.float32)
    return jnp.concatenate([x1 * cos - x2 * sin, x1 * sin + x2 * cos], -1).astype(x.dtype)


def rel_bucket(rel):
    nb = N_BUCKETS // 2
    max_exact = nb // 2
    n = jnp.abs(rel)
    large = max_exact + (jnp.log(jnp.maximum(n, 1).astype(jnp.float32) / max_exact)
                         / math.log(MAX_DISTANCE / max_exact) * (nb - max_exact)).astype(jnp.int32)
    large = jnp.minimum(large, nb - 1)
    return jnp.where(rel > 0, nb, 0) + jnp.where(n < max_exact, n, large)


def mla_attend(q_nope, q_rope, k_nope, k_rope, v):
    scale = (MLA_NOPE_DIM + MLA_ROPE_DIM) ** -0.5
    s = (jnp.einsum('bqhd,bkhd->bhqk', q_nope, k_nope)
         + jnp.einsum('bqhd,bkd->bhqk', q_rope, k_rope)).astype(jnp.float32) * scale
    p = jax.nn.softmax(s, axis=-1).astype(v.dtype)
    return jnp.einsum('bhqk,bkhd->bqhd', p, v)


def mla_branch(c_q, c_kv, k_rope_raw, q_norm_g, kv_norm_g, w_uq, w_ukv, cos, sin):
    bsz, n_tok, _ = c_q.shape
    q = (rms_norm(c_q, q_norm_g) @ w_uq).reshape(bsz, n_tok, MLA_HEADS, MLA_NOPE_DIM + MLA_ROPE_DIM)
    q_nope = q[..., :MLA_NOPE_DIM]
    q_rope = apply_rope(q[..., MLA_NOPE_DIM:], cos[None, :, None], sin[None, :, None])
    kv = (rms_norm(c_kv, kv_norm_g) @ w_ukv).reshape(bsz, n_tok, MLA_HEADS, MLA_NOPE_DIM + MLA_V_DIM)
    k_nope = kv[..., :MLA_NOPE_DIM]
    v = kv[..., MLA_NOPE_DIM:]
    k_rope = apply_rope(k_rope_raw, cos[None], sin[None])
    n_blk = (n_tok - N_META) // BLOCK

    def to_blocks(t):
        return jnp.moveaxis(t[:, N_META:].reshape(bsz, n_blk, BLOCK, *t.shape[2:]), 1, 0)

    o_meta = mla_attend(q_nope[:, :N_META], q_rope[:, :N_META], k_nope, k_rope, v)
    o_real = lax.map(lambda a: mla_attend(a[0], a[1], k_nope, k_rope, v),
                     (to_blocks(q_nope), to_blocks(q_rope)))
    o_real = jnp.moveaxis(o_real, 0, 1).reshape(bsz, n_tok - N_META, MLA_HEADS, MLA_V_DIM)
    return jnp.concatenate([o_meta, o_real], 1).reshape(bsz, n_tok, MLA_HEADS * MLA_V_DIM)


def swa_attend(q, k, v, q_pos, k_pos, always, in_range, rel_bias, sink):
    rel = k_pos[None, :] - q_pos[:, None]
    visible = always[None, :] | (in_range[None, :] & (jnp.abs(rel) <= WINDOW))
    n_q, n_k = rel.shape
    bias = jnp.transpose(rel_bias[rel_bucket(rel)], (2, 0, 1)).reshape(
        SWA_KV_HEADS, SWA_GROUP, n_q, n_k).astype(jnp.float32)
    s = jnp.einsum('bqhgd,bnhd->bhgqn', q, k).astype(jnp.float32) * SWA_HEAD_DIM ** -0.5 + bias
    s = jnp.where(visible, s, NEG)
    sink_col = jnp.broadcast_to(sink.astype(jnp.float32).reshape(SWA_KV_HEADS, SWA_GROUP, 1, 1),
                                s.shape[:-1] + (1,))
    p = jax.nn.softmax(jnp.concatenate([s, sink_col], -1), axis=-1)[..., :-1].astype(v.dtype)
    return jnp.einsum('bhgqn,bnhd->bqhgd', p, v)


def swa_branch(q, k, v, rel_bias, sink):
    bsz, n_tok, _ = q.shape
    n_real = n_tok - N_META
    n_blk = n_real // BLOCK
    q = q.reshape(bsz, n_tok, SWA_KV_HEADS, SWA_GROUP, SWA_HEAD_DIM)
    k = k.reshape(bsz, n_tok, SWA_KV_HEADS, SWA_HEAD_DIM)
    v = v.reshape(bsz, n_tok, SWA_KV_HEADS, SWA_HEAD_DIM)
    k_meta, v_meta = k[:, :N_META], v[:, :N_META]
    meta_pos = jnp.arange(N_META)
    mk_pos = jnp.arange(N_META + BLOCK)
    o_meta = swa_attend(q[:, :N_META], k[:, :N_META + BLOCK], v[:, :N_META + BLOCK], meta_pos, mk_pos,
                        mk_pos < N_META, jnp.ones((N_META + BLOCK,), bool), rel_bias, sink)
    pad = ((0, 0), (BLOCK, BLOCK), (0, 0), (0, 0))
    k_pad = jnp.pad(k[:, N_META:], pad)
    v_pad = jnp.pad(v[:, N_META:], pad)
    band = jnp.arange(3 * BLOCK)
    always = jnp.concatenate([jnp.ones((N_META,), bool), jnp.zeros((3 * BLOCK,), bool)])
    q_blocks = jnp.moveaxis(q[:, N_META:].reshape(bsz, n_blk, BLOCK, SWA_KV_HEADS, SWA_GROUP,
                                                  SWA_HEAD_DIM), 1, 0)

    def block(args):
        qb, i = args
        start = i * BLOCK
        kb = lax.dynamic_slice_in_dim(k_pad, start, 3 * BLOCK, axis=1)
        vb = lax.dynamic_slice_in_dim(v_pad, start, 3 * BLOCK, axis=1)
        r_key = start - BLOCK + band
        k_pos = jnp.concatenate([meta_pos, N_META + r_key])
        in_range = jnp.concatenate([jnp.ones((N_META,), bool), (r_key >= 0) & (r_key < n_real)])
        q_pos = N_META + start + jnp.arange(BLOCK)
        return swa_attend(qb, jnp.concatenate([k_meta, kb], 1), jnp.concatenate([v_meta, vb], 1),
                          q_pos, k_pos, always, in_range, rel_bias, sink)

    o_real = lax.map(block, (q_blocks, jnp.arange(n_blk)))
    o_real = jnp.moveaxis(o_real, 0, 1).reshape(bsz, n_real, SWA_KV_HEADS, SWA_GROUP, SWA_HEAD_DIM)
    return jnp.concatenate([o_meta, o_real], 1).reshape(bsz, n_tok, SWA_HEADS * SWA_HEAD_DIM)


def mixer(h, w_in, q_norm_g, kv_norm_g, w_uq, w_ukv, sink, w_proj_a, w_proj_b, w_out,
          rel_bias, cos, sin):
    z = h @ w_in
    points = [int(p) for p in np.cumsum(SPLITS)[:-1]]
    c_q, c_kv, k_r, q_b, k_b, v_b, g_a, g_b = jnp.split(z, points, axis=-1)
    o_a = mla_branch(c_q, c_kv, k_r, q_norm_g, kv_norm_g, w_uq, w_ukv, cos, sin)
    o_b = swa_branch(q_b, k_b, v_b, rel_bias, sink)
    merged = jax.nn.sigmoid(g_a) * (o_a @ w_proj_a) + jax.nn.sigmoid(g_b) * (o_b @ w_proj_b)
    return merged @ w_out


def dense_swiglu(h, w1, w3, w2):
    def per_seq(hb):
        return (jax.nn.silu(hb @ w1) * (hb @ w3)) @ w2
    return lax.map(per_seq, h)


def moe_swiglu(h, router_w, router_b, w1, w3, w2):
    bsz, n_tok, dm = h.shape
    n = bsz * n_tok
    xt = h.reshape(n, dm)
    logits = (xt @ router_w).astype(jnp.float32) + router_b.astype(jnp.float32)
    top_val, top_idx = lax.top_k(logits, TOP_K)
    gate = jax.nn.softmax(top_val, axis=-1)
    n_assign = n * TOP_K
    e_flat = top_idx.reshape(n_assign)
    tok_flat = jnp.arange(n_assign) // TOP_K
    g_flat = gate.reshape(n_assign)
    order = jnp.argsort(e_flat)
    e_sorted = e_flat[order]
    counts = jnp.bincount(e_flat, length=N_EXPERTS)
    padded = (counts + MOE_BLOCK - 1) // MOE_BLOCK * MOE_BLOCK
    pad_end = jnp.cumsum(padded)
    pad_start = pad_end - padded
    start = jnp.cumsum(counts) - counts
    dest = pad_start[e_sorted] + jnp.arange(n_assign) - start[e_sorted]
    n_blocks = -(-n_assign // MOE_BLOCK) + N_EXPERTS
    n_slots = n_blocks * MOE_BLOCK
    slot_tok = jnp.full((n_slots,), n, jnp.int32).at[dest].set(tok_flat[order].astype(jnp.int32))
    slot_gate = jnp.zeros((n_slots,), jnp.float32).at[dest].set(g_flat[order])
    block_expert = jnp.minimum(jnp.searchsorted(pad_end, jnp.arange(n_blocks) * MOE_BLOCK, side='right'),
                               N_EXPERTS - 1)
    x_pad = jnp.concatenate([xt, jnp.zeros((1, dm), xt.dtype)], 0)

    def expert_block(args):
        toks, e = args
        xb = x_pad[toks]
        return (jax.nn.silu(xb @ w1[e]) * (xb @ w3[e])) @ w2[e]

    y = lax.map(expert_block, (slot_tok.reshape(n_blocks, MOE_BLOCK), block_expert)).reshape(n_slots, dm)
    out = jax.ops.segment_sum(y * slot_gate[:, None].astype(y.dtype), slot_tok, num_segments=n + 1)[:n]
    return out.reshape(bsz, n_tok, dm)


def setup_inputs(seed: int = 0) -> dict:
    key = jax.random.key(seed)
    ks = jax.random.split(key, 26)

    def nrm(k, shape, scale):
        return jax.random.normal(k, shape, jnp.float32) * scale

    d = D_MODEL
    return {
        'x': nrm(ks[0], (BATCH, SEQ, d), 1.0),
        'meta_tokens': nrm(ks[1], (N_META, d), 1.0),
        'emb_ln_g': 1.0 + nrm(ks[2], (d,), 0.1),
        'emb_ln_b': nrm(ks[3], (d,), 0.02),
        'rel_bias': nrm(ks[4], (N_BUCKETS, SWA_HEADS), 0.5),
        'w_in': nrm(ks[5], (DEPTH, d, IN_DIM), d ** -0.5),
        'q_norm_g': 1.0 + nrm(ks[6], (DEPTH, MLA_Q_RANK), 0.1),
        'kv_norm_g': 1.0 + nrm(ks[7], (DEPTH, MLA_KV_RANK), 0.1),
        'w_uq': nrm(ks[8], (DEPTH, MLA_Q_RANK, MLA_HEADS * (MLA_NOPE_DIM + MLA_ROPE_DIM)), MLA_Q_RANK ** -0.5),
        'w_ukv': nrm(ks[9], (DEPTH, MLA_KV_RANK, MLA_HEADS * (MLA_NOPE_DIM + MLA_V_DIM)), MLA_KV_RANK ** -0.5),
        'sink_logits': nrm(ks[10], (DEPTH, SWA_HEADS), 0.5),
        'w_proj_a': nrm(ks[11], (DEPTH, MLA_HEADS * MLA_V_DIM, d), BETA * (MLA_HEADS * MLA_V_DIM) ** -0.5),
        'w_proj_b': nrm(ks[12], (DEPTH, SWA_HEADS * SWA_HEAD_DIM, d), BETA * (SWA_HEADS * SWA_HEAD_DIM) ** -0.5),
        'w_out': nrm(ks[13], (DEPTH, d, d), BETA * d ** -0.5),
        'ln_mix_g': 1.0 + nrm(ks[14], (DEPTH, d), 0.1),
        'ln_mix_b': nrm(ks[15], (DEPTH, d), 0.02),
        'ln_ffn_g': 1.0 + nrm(ks[16], (DEPTH, d), 0.1),
        'ln_ffn_b': nrm(ks[17], (DEPTH, d), 0.02),
        'ffn_w1': nrm(ks[18], (N_DENSE, d, D_FF), d ** -0.5),
        'ffn_w3': nrm(ks[19], (N_DENSE, d, D_FF), d ** -0.5),
        'ffn_w2': nrm(ks[20], (N_DENSE, D_FF, d), BETA * D_FF ** -0.5),
        'router_w': nrm(ks[21], (N_MOE, d, N_EXPERTS), d ** -0.5),
        'router_b': nrm(ks[22], (N_MOE, N_EXPERTS), 0.01),
        'moe_w1': nrm(ks[23], (N_MOE, N_EXPERTS, d, D_FF), d ** -0.5),
        'moe_w3': nrm(ks[24], (N_MOE, N_EXPERTS, d, D_FF), d ** -0.5),
        'moe_w2': nrm(ks[25], (N_MOE, N_EXPERTS, D_FF, d), BETA * D_FF ** -0.5),
    }


def reference(x, meta_tokens, emb_ln_g, emb_ln_b, rel_bias, w_in, q_norm_g, kv_norm_g, w_uq, w_ukv,
              sink_logits, w_proj_a, w_proj_b, w_out, ln_mix_g, ln_mix_b, ln_ffn_g, ln_ffn_b,
              ffn_w1, ffn_w3, ffn_w2, router_w, router_b, moe_w1, moe_w3, moe_w2):
    bsz = x.shape[0]
    meta = jnp.broadcast_to(meta_tokens[None].astype(x.dtype), (bsz, N_META, x.shape[-1]))
    h = layer_norm(jnp.concatenate([meta, x], 1), emb_ln_g, emb_ln_b)
    cos, sin = rope_tables(h.shape[1])
    for l in range(DEPTH):
        m = mixer(h, w_in[l], q_norm_g[l], kv_norm_g[l], w_uq[l], w_ukv[l], sink_logits[l],
                  w_proj_a[l], w_proj_b[l], w_out[l], rel_bias, cos, sin)
        h = layer_norm(ALPHA * h + m, ln_mix_g[l], ln_mix_b[l])
        if l % 2 == 0:
            f = dense_swiglu(h, ffn_w1[l // 2], ffn_w3[l // 2], ffn_w2[l // 2])
        else:
            f = moe_swiglu(h, router_w[l // 2], router_b[l // 2], moe_w1[l // 2], moe_w3[l // 2],
                           moe_w2[l // 2])
        h = layer_norm(ALPHA * h + f, ln_ffn_g[l], ln_ffn_b[l])
    return h[:, N_META:]
```

```python
import functools
import math

import jax
import jax.numpy as jnp
from jax import lax
from jax.experimental import pallas as pl
from jax.experimental.pallas import tpu as pltpu

MLA_HEADS = 16
MLA_NOPE_DIM = 128
MLA_ROPE_DIM = 64
MLA_V_DIM = 128
SWA_HEADS = 16
SWA_KV_HEADS = 4
SWA_HEAD_DIM = 128
WINDOW = 128
BLOCK = 128
N_BUCKETS = 32
MAX_DISTANCE = 128
TOP_K = 2
ROPE_THETA = 10000.0
LN_EPS = 1e-5
RMS_EPS = 1e-6
NEG = -1e30

V7X_LANES = 128
V7X_MXU_DIM = 256
V7X_VMEM_LIMIT_BYTES = 56 * 1024 * 1024

F32 = jnp.float32
BF16 = jnp.bfloat16


COMPILER_SCRATCH_BYTES = 8 * 1024 * 1024


def _cparams(semantics, vmem_bytes):
    limit = int(min(V7X_VMEM_LIMIT_BYTES, vmem_bytes + COMPILER_SCRATCH_BYTES))
    return pltpu.CompilerParams(dimension_semantics=semantics, vmem_limit_bytes=limit)


def _nbytes(shape, dtype):
    return math.prod(shape) * jnp.dtype(dtype).itemsize


def _dot(a, b):
    return jnp.dot(a, b, preferred_element_type=F32)


def _dot_t(a, b):
    return lax.dot_general(a, b, (((1,), (1,)), ((), ())), preferred_element_type=F32)


def _tile(n, pref):
    t = min(n, pref)
    while n % t:
        t //= 2
    return t


def _ln_rows(y, g, b):
    mu = jnp.mean(y, axis=-1, keepdims=True)
    d = y - mu
    var = jnp.mean(d * d, axis=-1, keepdims=True)
    return d * lax.rsqrt(var + LN_EPS) * g + b


def _embed_ln_kernel(x_ref, g_ref, b_ref, h_ref, hb_ref):
    y = _ln_rows(x_ref[...], g_ref[...], b_ref[...])
    h_ref[...] = y
    hb_ref[...] = y.astype(BF16)


def _embed_ln(x, g, b):
    m, d = x.shape
    tm = _tile(m, 512)
    row = pl.BlockSpec((tm, d), lambda i: (i, 0))
    vec = pl.BlockSpec((1, d), lambda i: (0, 0))
    return pl.pallas_call(
        _embed_ln_kernel,
        grid=(m // tm,),
        in_specs=[row, vec, vec],
        out_specs=[row, row],
        out_shape=[jax.ShapeDtypeStruct((m, d), F32), jax.ShapeDtypeStruct((m, d), BF16)],
        compiler_params=_cparams(("parallel",), 2 * tm * d * 10 + 4 * tm * d * 4),
        name="embed_ln",
    )(x, g.reshape(1, d), b.reshape(1, d))


def _mm_kernel(a_ref, w_ref, o_ref):
    o_ref[...] = _dot(a_ref[...], w_ref[...]).astype(o_ref.dtype)


def _matmul(a, w, out_dtype, tm_pref=1024, tn_pref=768):
    m, k = a.shape
    n = w.shape[1]
    tm = _tile(m, tm_pref)
    tn = next(t for t in (tn_pref, 512, 256, 128) if n % t == 0)
    vm = 2 * (_nbytes((tm, k), BF16) + _nbytes((k, tn), BF16) + _nbytes((tm, tn), out_dtype)) + _nbytes((tm, tn), F32)
    return pl.pallas_call(
        _mm_kernel,
        grid=(m // tm, n // tn),
        in_specs=[pl.BlockSpec((tm, k), lambda i, j: (i, 0)), pl.BlockSpec((k, tn), lambda i, j: (0, j))],
        out_specs=pl.BlockSpec((tm, tn), lambda i, j: (i, j)),
        out_shape=jax.ShapeDtypeStruct((m, n), out_dtype),
        compiler_params=_cparams(("parallel", "parallel"), vm),
        name="in_proj",
    )(a, w)


def _rope(x, ct, st):
    return x * ct + pltpu.roll(x, MLA_ROPE_DIM // 2, 1) * st


def _rms(c, g):
    return c * lax.rsqrt(jnp.mean(c * c, axis=-1, keepdims=True) + RMS_EPS) * g


def _q_proj_kernel(c_ref, g_ref, wn_ref, wr_ref, ct_ref, st_ref, o_ref, *, heads, scale):
    cb = _rms(c_ref[...].astype(F32), g_ref[...]).astype(BF16)
    qn = _dot(cb, wn_ref[...])
    qr = _dot(cb, wr_ref[...])
    ct = ct_ref[...]
    st = st_ref[...]
    w = MLA_NOPE_DIM + V7X_LANES
    for h in range(heads):
        o_ref[:, h * w:h * w + MLA_NOPE_DIM] = (qn[:, h * MLA_NOPE_DIM:(h + 1) * MLA_NOPE_DIM] * scale).astype(BF16)
        xr = qr[:, h * V7X_LANES:(h + 1) * V7X_LANES]
        o_ref[:, h * w + MLA_NOPE_DIM:(h + 1) * w] = (_rope(xr, ct, st) * scale).astype(BF16)


def _q_proj(z, cq_off, g, wn, wr, ct, st):
    m = z.shape[0]
    r = g.shape[0]
    heads = MLA_HEADS
    tm = _tile(math.gcd(m, ct.shape[0]), 512)
    npos = ct.shape[0] // tm
    assert cq_off % r == 0
    wout = heads * (MLA_NOPE_DIM + V7X_LANES)
    scale = (MLA_NOPE_DIM + MLA_ROPE_DIM) ** -0.5
    vm = (2 * (_nbytes((tm, r), BF16) + _nbytes(wn.shape, BF16) + _nbytes(wr.shape, BF16) + 2 * tm * V7X_LANES * 4
               + _nbytes((tm, wout), BF16)) + 2 * _nbytes((tm, wn.shape[1]), F32) + _nbytes((tm, r), F32) * 2)
    tab = pl.BlockSpec((tm, V7X_LANES), lambda i: (i % npos, 0))
    return pl.pallas_call(
        functools.partial(_q_proj_kernel, heads=heads, scale=scale),
        grid=(m // tm,),
        in_specs=[pl.BlockSpec((tm, r), lambda i: (i, cq_off // r)),
                  pl.BlockSpec((1, r), lambda i: (0, 0)),
                  pl.BlockSpec(wn.shape, lambda i: (0, 0)),
                  pl.BlockSpec(wr.shape, lambda i: (0, 0)),
                  tab, tab],
        out_specs=pl.BlockSpec((tm, wout), lambda i: (i, 0)),
        out_shape=jax.ShapeDtypeStruct((m, wout), BF16),
        compiler_params=_cparams(("parallel",), vm),
        name="mla_q_proj",
    )(z, g.reshape(1, r), wn, wr, ct, st)


def _kv_proj_kernel(hb_ref, c_ref, g_ref, wk_ref, wv_ref, wkr_ref, ct_ref, st_ref, k_ref, v_ref, *, heads):
    cb = _rms(c_ref[...].astype(F32), g_ref[...]).astype(BF16)
    kn = _dot(cb, wk_ref[...])
    v_ref[...] = _dot(cb, wv_ref[...]).astype(BF16)
    kr = _rope(_dot(hb_ref[...], wkr_ref[...]), ct_ref[...], st_ref[...]).astype(BF16)
    w = MLA_NOPE_DIM + V7X_LANES
    for h in range(heads):
        k_ref[:, h * w:h * w + MLA_NOPE_DIM] = kn[:, h * MLA_NOPE_DIM:(h + 1) * MLA_NOPE_DIM].astype(BF16)
        k_ref[:, h * w + MLA_NOPE_DIM:(h + 1) * w] = kr


def _kv_proj(hb, z, ckv_off, g, wk, wv, wkr, ct, st):
    m, d = hb.shape
    r = g.shape[0]
    heads = MLA_HEADS
    tm = _tile(math.gcd(m, ct.shape[0]), 512)
    npos = ct.shape[0] // tm
    assert ckv_off % r == 0
    wk_out = heads * (MLA_NOPE_DIM + V7X_LANES)
    wv_out = heads * MLA_V_DIM
    vm = (2 * (_nbytes((tm, d), BF16) + _nbytes((tm, r), BF16) + _nbytes(wk.shape, BF16) + _nbytes(wv.shape, BF16)
               + _nbytes(wkr.shape, BF16) + 2 * tm * V7X_LANES * 4 + _nbytes((tm, wk_out), BF16)
               + _nbytes((tm, wv_out), BF16)) + 2 * _nbytes((tm, wv_out), F32) + _nbytes((tm, r), F32) * 2)
    tab = pl.BlockSpec((tm, V7X_LANES), lambda i: (i % npos, 0))
    return pl.pallas_call(
        functools.partial(_kv_proj_kernel, heads=heads),
        grid=(m // tm,),
        in_specs=[pl.BlockSpec((tm, d), lambda i: (i, 0)),
                  pl.BlockSpec((tm, r), lambda i: (i, ckv_off // r)),
                  pl.BlockSpec((1, r), lambda i: (0, 0)),
                  pl.BlockSpec(wk.shape, lambda i: (0, 0)),
                  pl.BlockSpec(wv.shape, lambda i: (0, 0)),
                  pl.BlockSpec(wkr.shape, lambda i: (0, 0)),
                  tab, tab],
        out_specs=[pl.BlockSpec((tm, wk_out), lambda i: (i, 0)), pl.BlockSpec((tm, wv_out), lambda i: (i, 0))],
        out_shape=[jax.ShapeDtypeStruct((m, wk_out), BF16), jax.ShapeDtypeStruct((m, wv_out), BF16)],
        compiler_params=_cparams(("parallel",), vm),
        name="mla_kv_proj",
    )(hb, z, g.reshape(1, r), wk, wv, wkr, ct, st)


def _mla_attn_kernel(q_ref, k_ref, km_ref, v_ref, vm_ref, o_ref, *, tk, n_meta):
    q = q_ref[...]
    tq = q.shape[0]
    n_chunks = k_ref.shape[0] // tk
    pad = V7X_LANES - n_meta
    kmp = jnp.concatenate([km_ref[...], jnp.zeros((pad, km_ref.shape[1]), BF16)], axis=0)
    vmp = jnp.concatenate([vm_ref[...], jnp.zeros((pad, vm_ref.shape[1]), BF16)], axis=0)
    s = _dot_t(q, kmp)
    lane = lax.broadcasted_iota(jnp.int32, s.shape, 1)
    s = jnp.where(lane < n_meta, s, NEG)
    m0 = jnp.max(s, axis=-1, keepdims=True)
    p = jnp.exp(s - m0)
    l0 = jnp.sum(p, axis=-1, keepdims=True)
    acc0 = _dot(p.astype(BF16), vmp)

    def body(c, carry):
        m, l, acc = carry
        start = pl.multiple_of(c * tk, tk)
        kc = k_ref[pl.ds(start, tk), :]
        vc = v_ref[pl.ds(start, tk), :]
        s = _dot_t(q, kc)
        m_new = jnp.maximum(m, jnp.max(s, axis=-1, keepdims=True))
        a = jnp.exp(m - m_new)
        p = jnp.exp(s - m_new)
        l = a * l + jnp.sum(p, axis=-1, keepdims=True)
        acc = a * acc + _dot(p.astype(BF16), vc)
        return m_new, l, acc

    m, l, acc = lax.fori_loop(0, n_chunks, body, (m0, l0, acc0))
    o_ref[...] = (acc / l).astype(o_ref.dtype)


def _mla_attn(q, k, km, v, vmeta, batch, n_meta):
    heads = MLA_HEADS
    wq = MLA_NOPE_DIM + V7X_LANES
    rows_q = q.shape[0] // batch
    seq = k.shape[0] // batch
    tq = _tile(rows_q, 512)
    tk = _tile(seq, 512)
    nq = rows_q // tq
    vm = (2 * (_nbytes((tq, wq), BF16) + _nbytes((seq, wq), BF16) + _nbytes((seq, MLA_V_DIM), BF16)
               + _nbytes((tq, MLA_V_DIM), BF16)) + 6 * _nbytes((tq, tk), F32))
    return pl.pallas_call(
        functools.partial(_mla_attn_kernel, tk=tk, n_meta=n_meta),
        grid=(batch, heads, nq),
        in_specs=[pl.BlockSpec((tq, wq), lambda b, h, i: (b * nq + i, h)),
                  pl.BlockSpec((seq, wq), lambda b, h, i: (b, h)),
                  pl.BlockSpec((n_meta, wq), lambda b, h, i: (b, h)),
                  pl.BlockSpec((seq, MLA_V_DIM), lambda b, h, i: (b, h)),
                  pl.BlockSpec((n_meta, MLA_V_DIM), lambda b, h, i: (b, h))],
        out_specs=pl.BlockSpec((tq, MLA_V_DIM), lambda b, h, i: (b * nq + i, h)),
        out_shape=jax.ShapeDtypeStruct((q.shape[0], heads * MLA_V_DIM), BF16),
        compiler_params=_cparams(("parallel", "parallel", "parallel"), vm),
        name="mla_attn",
    )(q, k, km, v, vmeta)


def _bias_kernel(rb_ref, bk_ref, vis_ref, o_ref, *, n_buckets):
    h = pl.program_id(1)
    bk = bk_ref[0]
    acc = jnp.zeros(bk.shape, F32)
    for j in range(n_buckets):
        acc = jnp.where(bk == j, rb_ref[j, h], acc)
    o_ref[0, 0] = jnp.where(vis_ref[0] != 0, acc, NEG)


def _bias_tables(rel_bias, bucket, vis):
    t, r, s = bucket.shape
    heads = rel_bias.shape[1]
    blk = pl.BlockSpec((1, r, s), lambda i, h: (i, 0, 0))
    return pl.pallas_call(
        functools.partial(_bias_kernel, n_buckets=rel_bias.shape[0]),
        grid=(t, heads),
        in_specs=[pl.BlockSpec(memory_space=pltpu.SMEM), blk, blk],
        out_specs=pl.BlockSpec((1, 1, r, s), lambda i, h: (i, h, 0, 0)),
        out_shape=jax.ShapeDtypeStruct((t, heads, r, s), F32),
        compiler_params=_cparams(("parallel", "parallel"), 16 * r * s * 4),
        name="swa_bias_table",
    )(rel_bias, bucket, vis)


def _swa_kernel(sink_ref, q_ref, kp_ref, ko_ref, kn_ref, km_ref, vp_ref, vo_ref, vn_ref, vm_ref, bias_ref, o_ref,
                *, group, scale, n_meta):
    g = pl.program_id(1)
    d = SWA_HEAD_DIM
    zpad = jnp.zeros((BLOCK - n_meta, d), BF16)
    kcat = jnp.concatenate([kp_ref[...], ko_ref[...], kn_ref[...], km_ref[...], zpad], axis=0)
    vcat = jnp.concatenate([vp_ref[...], vo_ref[...], vn_ref[...], vm_ref[...], zpad], axis=0)
    for j in range(group):
        s = _dot_t(q_ref[:, j * d:(j + 1) * d], kcat) * scale + bias_ref[0, j]
        sink = sink_ref[g * group + j]
        m = jnp.maximum(jnp.max(s, axis=-1, keepdims=True), sink)
        p = jnp.exp(s - m)
        denom = jnp.sum(p, axis=-1, keepdims=True) + jnp.exp(sink - m)
        o = _dot(p.astype(BF16), vcat) / denom
        o_ref[:, j * d:(j + 1) * d] = o.astype(o_ref.dtype)


def _swa_attn(sink, q_src, q_rows, kv_src, kv_meta, k_off, v_off, bias, type_of_block, kv_block_of, batch, n_grid_blk,
              n_meta):
    d = SWA_HEAD_DIM
    group = SWA_HEADS // SWA_KV_HEADS
    nblk_kv = kv_src.shape[0] // batch // BLOCK
    s_keys = bias.shape[-1]
    kc, vc = k_off // d, v_off // d

    def kv_spec(which, col0):
        return pl.BlockSpec((BLOCK, d), lambda b, g, i: (b * nblk_kv + kv_block_of(i)[which], col0 + g))

    def meta_spec(col0):
        return pl.BlockSpec((n_meta, d), lambda b, g, i: (b, col0 + g))

    vm = 2 * (q_rows * group * d * 2 * 2 + 8 * BLOCK * d * 2 + group * q_rows * s_keys * 4) + 8 * q_rows * s_keys * 4
    return pl.pallas_call(
        functools.partial(_swa_kernel, group=group, scale=d ** -0.5, n_meta=n_meta),
        grid=(batch, SWA_KV_HEADS, n_grid_blk),
        in_specs=[pl.BlockSpec(memory_space=pltpu.SMEM),
                  pl.BlockSpec((q_rows, group * d), lambda b, g, i: (b * n_grid_blk + i, g)),
                  kv_spec(0, kc), kv_spec(1, kc), kv_spec(2, kc), meta_spec(kc),
                  kv_spec(0, vc), kv_spec(1, vc), kv_spec(2, vc), meta_spec(vc),
                  pl.BlockSpec((1, group, q_rows, s_keys), lambda b, g, i: (type_of_block(i), g, 0, 0))],
        out_specs=pl.BlockSpec((q_rows, group * d), lambda b, g, i: (b * n_grid_blk + i, g)),
        out_shape=jax.ShapeDtypeStruct((batch * n_grid_blk * q_rows, SWA_HEADS * d), BF16),
        compiler_params=_cparams(("parallel", "parallel", "parallel"), vm),
        name="swa_attn",
    )(sink, q_src, kv_src, kv_src, kv_src, kv_meta, kv_src, kv_src, kv_src, kv_meta, bias)


def _gate_proj_kernel(oa_ref, ob_ref, ga_ref, gb_ref, wa_ref, wb_ref, o_ref):
    pa = _dot(oa_ref[...], wa_ref[...])
    pb = _dot(ob_ref[...], wb_ref[...])
    ga = jax.nn.sigmoid(ga_ref[...].astype(F32))
    gb = jax.nn.sigmoid(gb_ref[...].astype(F32))
    o_ref[...] = (ga * pa + gb * pb).astype(o_ref.dtype)


def _gate_proj(oa, ob, z, ga_off, gb_off, wa, wb):
    m, ka = oa.shape
    kb = ob.shape[1]
    n = wa.shape[1]
    tm = _tile(m, 1024)
    tn = _tile(n, 512)
    assert ga_off % tn == 0 and gb_off % tn == 0
    vm = (2 * (_nbytes((tm, ka + kb), BF16) + _nbytes((ka + kb, tn), BF16) + 3 * _nbytes((tm, tn), BF16))
          + 4 * _nbytes((tm, tn), F32))
    act = lambda k: pl.BlockSpec((tm, k), lambda i, j: (i, 0))
    wsp = lambda k: pl.BlockSpec((k, tn), lambda i, j: (0, j))
    return pl.pallas_call(
        _gate_proj_kernel,
        grid=(m // tm, n // tn),
        in_specs=[act(ka), act(kb),
                  pl.BlockSpec((tm, tn), lambda i, j: (i, ga_off // tn + j)),
                  pl.BlockSpec((tm, tn), lambda i, j: (i, gb_off // tn + j)),
                  wsp(ka), wsp(kb)],
        out_specs=pl.BlockSpec((tm, tn), lambda i, j: (i, j)),
        out_shape=jax.ShapeDtypeStruct((m, n), BF16),
        compiler_params=_cparams(("parallel", "parallel"), vm),
        name="gate_proj",
    )(oa, ob, z, z, wa, wb)


def _out_ln_kernel(m_ref, w_ref, h_ref, g_ref, b_ref, ho_ref, hbo_ref, *, alpha):
    y = alpha * h_ref[...] + _dot(m_ref[...], w_ref[...])
    y = _ln_rows(y, g_ref[...], b_ref[...])
    ho_ref[...] = y
    hbo_ref[...] = y.astype(BF16)


def _out_ln(merged, w, h, g, b, alpha):
    m, d = h.shape
    tm = _tile(m, 256)
    row = lambda dt: pl.BlockSpec((tm, d), lambda i: (i, 0))
    vec = pl.BlockSpec((1, d), lambda i: (0, 0))
    vm = 2 * (_nbytes((tm, d), BF16) * 2 + _nbytes(w.shape, BF16) + 2 * _nbytes((tm, d), F32)) + 3 * _nbytes((tm, d), F32)
    return pl.pallas_call(
        functools.partial(_out_ln_kernel, alpha=alpha),
        grid=(m // tm,),
        in_specs=[row(BF16), pl.BlockSpec(w.shape, lambda i: (0, 0)), row(F32), vec, vec],
        out_specs=[row(F32), row(BF16)],
        out_shape=[jax.ShapeDtypeStruct((m, d), F32), jax.ShapeDtypeStruct((m, d), BF16)],
        compiler_params=_cparams(("parallel",), vm),
        name="out_proj_ln",
    )(merged, w, h, g.reshape(1, d), b.reshape(1, d))


def _swiglu(a1, a3):
    return a1 * jax.nn.sigmoid(a1) * a3


def _ffn_up_kernel(h_ref, w1_ref, w3_ref, o_ref):
    h = h_ref[...]
    o_ref[...] = _swiglu(_dot(h, w1_ref[...]), _dot(h, w3_ref[...])).astype(o_ref.dtype)


def _ffn_up(hb, w1, w3):
    m, d = hb.shape
    f = w1.shape[1]
    tm = _tile(m, 1024)
    tf = _tile(f, 512)
    vm = 2 * (_nbytes((tm, d), BF16) + 2 * _nbytes((d, tf), BF16) + _nbytes((tm, tf), BF16)) + 4 * _nbytes((tm, tf), F32)
    wsp = pl.BlockSpec((d, tf), lambda i, j: (0, j))
    return pl.pallas_call(
        _ffn_up_kernel,
        grid=(m // tm, f // tf),
        in_specs=[pl.BlockSpec((tm, d), lambda i, j: (i, 0)), wsp, wsp],
        out_specs=pl.BlockSpec((tm, tf), lambda i, j: (i, j)),
        out_shape=jax.ShapeDtypeStruct((m, f), BF16),
        compiler_params=_cparams(("parallel", "parallel"), vm),
        name="ffn_up",
    )(hb, w1, w3)


def _ffn_down_ln_kernel(g_ref, w2_ref, h_ref, lg_ref, lb_ref, ho_ref, hbo_ref, acc_ref, *, alpha):
    k = pl.program_id(1)

    @pl.when(k == 0)
    def _():
        acc_ref[...] = jnp.zeros_like(acc_ref)

    acc_ref[...] += _dot(g_ref[...], w2_ref[...])

    @pl.when(k == pl.num_programs(1) - 1)
    def _():
        y = _ln_rows(alpha * h_ref[...] + acc_ref[...], lg_ref[...], lb_ref[...])
        ho_ref[...] = y
        hbo_ref[...] = y.astype(BF16)


def _ffn_down_ln(gact, w2, h, lg, lb, alpha):
    m, d = h.shape
    f = w2.shape[0]
    tm = _tile(m, 512)
    tk = _tile(f, 512)
    vm = (2 * (_nbytes((tm, tk), BF16) + _nbytes((tk, d), BF16) + 2 * _nbytes((tm, d), F32) + _nbytes((tm, d), BF16))
          + 3 * _nbytes((tm, d), F32))
    row = pl.BlockSpec((tm, d), lambda i, k: (i, 0))
    vec = pl.BlockSpec((1, d), lambda i, k: (0, 0))
    return pl.pallas_call(
        functools.partial(_ffn_down_ln_kernel, alpha=alpha),
        grid=(m // tm, f // tk),
        in_specs=[pl.BlockSpec((tm, tk), lambda i, k: (i, k)), pl.BlockSpec((tk, d), lambda i, k: (k, 0)), row, vec, vec],
        out_specs=[row, row],
        out_shape=[jax.ShapeDtypeStruct((m, d), F32), jax.ShapeDtypeStruct((m, d), BF16)],
        scratch_shapes=[pltpu.VMEM((tm, d), F32)],
        compiler_params=_cparams(("parallel", "arbitrary"), vm),
        name="ffn_down_ln",
    )(gact, w2, h, lg.reshape(1, d), lb.reshape(1, d))


_PK_E1, _PK_E2, _PK_G1, _PK_G2, _PK_R1, _PK_R2 = range(6)


def _router_kernel(h_ref, w_ref, b_ref, pk_ref, cnt_ref, carry_ref, *, n_exp):
    @pl.when(pl.program_id(0) == 0)
    def _():
        carry_ref[...] = jnp.zeros_like(carry_ref)

    logits = jnp.dot(h_ref[...], w_ref[...], preferred_element_type=F32, precision=lax.Precision.HIGHEST) + b_ref[...]
    tm, nl = logits.shape
    lane = lax.broadcasted_iota(jnp.int32, (tm, nl), 1).astype(F32)
    s = jnp.where(lane < n_exp, logits, -jnp.inf)
    m1 = jnp.max(s, axis=-1, keepdims=True)
    i1 = jnp.min(jnp.where(s == m1, lane, float(nl)), axis=-1, keepdims=True)
    s2 = jnp.where(lane == i1, -jnp.inf, s)
    m2 = jnp.max(s2, axis=-1, keepdims=True)
    i2 = jnp.min(jnp.where(s2 == m2, lane, float(nl)), axis=-1, keepdims=True)
    e = jnp.exp(m2 - m1)
    g1 = 1.0 / (1.0 + e)
    g2 = e / (1.0 + e)
    oh1 = lane == i1
    oh2 = lane == i2
    both = jnp.where(oh1 | oh2, 1.0, 0.0)
    r = lax.broadcasted_iota(jnp.int32, (tm, tm), 0)
    c = lax.broadcasted_iota(jnp.int32, (tm, tm), 1)
    tri = jnp.where(c < r, 1.0, 0.0).astype(BF16)
    prefix = _dot(tri, both.astype(BF16)) + carry_ref[...]
    r1 = jnp.sum(jnp.where(oh1, prefix, 0.0), axis=-1, keepdims=True)
    r2 = jnp.sum(jnp.where(oh2, prefix, 0.0), axis=-1, keepdims=True)
    carry_ref[...] += jnp.sum(both, axis=0, keepdims=True)
    cnt_ref[...] = carry_ref[...]
    pk = jnp.zeros((tm, nl), F32)
    for col, val in ((_PK_E1, i1), (_PK_E2, i2), (_PK_G1, g1), (_PK_G2, g2), (_PK_R1, r1), (_PK_R2, r2)):
        pk = jnp.where(lane == col, val, pk)
    pk_ref[...] = pk


def _router(h, w_pad, b_pad, n_exp):
    m, d = h.shape
    nl = w_pad.shape[1]
    tm = _tile(m, 512)
    vm = 2 * (_nbytes((tm, d), F32) + _nbytes((d, nl), F32) + _nbytes((tm, nl), F32)) + _nbytes((tm, tm), F32) * 3
    return pl.pallas_call(
        functools.partial(_router_kernel, n_exp=n_exp),
        grid=(m // tm,),
        in_specs=[pl.BlockSpec((tm, d), lambda i: (i, 0)), pl.BlockSpec((d, nl), lambda i: (0, 0)),
                  pl.BlockSpec((1, nl), lambda i: (0, 0))],
        out_specs=[pl.BlockSpec((tm, nl), lambda i: (i, 0)), pl.BlockSpec((1, nl), lambda i: (0, 0))],
        out_shape=[jax.ShapeDtypeStruct((m, nl), F32), jax.ShapeDtypeStruct((1, nl), F32)],
        scratch_shapes=[pltpu.VMEM((1, nl), F32)],
        compiler_params=_cparams(("arbitrary",), vm),
        name="moe_router",
    )(h, w_pad, b_pad)


def _row_copy(src_ref, src_row, dst_ref, dst_row, sem):
    return pltpu.make_async_copy(src_ref.at[pl.ds(src_row, 1)], dst_ref.at[pl.ds(dst_row, 1)], sem)


def _dispatch_kernel(d1_ref, d2_ref, pad_ref, h_ref, zero_ref, xs_ref, sem, *, n_ranges):
    tb = h_ref.shape[0]

    def issue(t, carry):
        _row_copy(h_ref, t, xs_ref, d1_ref[0, 0, t], sem).start()
        _row_copy(h_ref, t, xs_ref, d2_ref[0, 0, t], sem).start()
        return carry

    def drain(t, carry):
        _row_copy(h_ref, 0, xs_ref, 0, sem).wait()
        _row_copy(h_ref, 0, xs_ref, 0, sem).wait()
        return carry

    lax.fori_loop(0, tb, issue, 0)

    @pl.when(pl.program_id(0) == 0)
    def _():
        for e in range(n_ranges):
            base = pad_ref[0, e]
            n = pad_ref[1, e]
            lax.fori_loop(0, n, lambda j, c: (_row_copy(zero_ref, 0, xs_ref, base + j, sem).start(), c)[1], 0)
            lax.fori_loop(0, n, lambda j, c: (_row_copy(zero_ref, 0, xs_ref, 0, sem).wait(), c)[1], 0)

    lax.fori_loop(0, tb, drain, 0)


def _dispatch(h, d1, d2, pad_info, n_slots):
    m, d = h.shape
    tb = _tile(m, 256)
    nt = m // tb
    idx = pl.BlockSpec((1, 1, tb), lambda i: (i, 0, 0), memory_space=pltpu.SMEM)
    zero = jnp.zeros((8, d), F32)
    return pl.pallas_call(
        functools.partial(_dispatch_kernel, n_ranges=pad_info.shape[1]),
        grid=(nt,),
        in_specs=[idx, idx, pl.BlockSpec(memory_space=pltpu.SMEM), pl.BlockSpec((tb, d), lambda i: (i, 0)),
                  pl.BlockSpec((8, d), lambda i: (0, 0))],
        out_specs=pl.BlockSpec(memory_space=pl.ANY),
        out_shape=jax.ShapeDtypeStruct((n_slots, d), F32),
        scratch_shapes=[pltpu.SemaphoreType.DMA(())],
        compiler_params=_cparams(("arbitrary",), 4 * tb * d * 4),
        name="moe_dispatch",
    )(d1.reshape(nt, 1, tb), d2.reshape(nt, 1, tb), pad_info, h, zero)


def _moe_ffn_kernel(te_ref, ts_ref, tv_ref, x_ref, w1_ref, w3_ref, w2_ref, y_ref, xb_ref):
    i = pl.program_id(0)
    j = pl.program_id(1)

    @pl.when(tv_ref[i] != 0)
    def _():
        @pl.when(j == 0)
        def _():
            xb_ref[...] = x_ref[...].astype(BF16)
            y_ref[...] = jnp.zeros_like(y_ref)

        xb = xb_ref[...]
        gact = _swiglu(_dot(xb, w1_ref[0]), _dot(xb, w3_ref[0])).astype(BF16)
        y_ref[...] += _dot(gact, w2_ref[0])

    @pl.when((tv_ref[i] == 0) & (j == 0))
    def _():
        y_ref[...] = jnp.zeros_like(y_ref)


def _moe_ffn(xs, w1, w3, w2, tile_expert, tile_src, tile_valid, tm):
    n_slots, d = xs.shape
    f = w1.shape[2]
    tf = _tile(f, 512)
    nf = f // tf
    n_tiles = n_slots // tm

    def jj(i, j, tv):
        return jnp.where(tv[i] != 0, j, nf - 1)

    vm = (2 * (2 * _nbytes((tm, d), F32) + 3 * _nbytes((d, tf), BF16)) + _nbytes((tm, d), BF16)
          + 4 * _nbytes((tm, tf), F32) + _nbytes((tm, d), F32))
    grid_spec = pltpu.PrefetchScalarGridSpec(
        num_scalar_prefetch=3,
        grid=(n_tiles, nf),
        in_specs=[pl.BlockSpec((tm, d), lambda i, j, te, ts, tv: (ts[i], 0)),
                  pl.BlockSpec((1, d, tf), lambda i, j, te, ts, tv: (te[i], 0, jj(i, j, tv))),
                  pl.BlockSpec((1, d, tf), lambda i, j, te, ts, tv: (te[i], 0, jj(i, j, tv))),
                  pl.BlockSpec((1, tf, d), lambda i, j, te, ts, tv: (te[i], jj(i, j, tv), 0))],
        out_specs=pl.BlockSpec((tm, d), lambda i, j, te, ts, tv: (i, 0)),
        scratch_shapes=[pltpu.VMEM((tm, d), BF16)],
    )
    return pl.pallas_call(
        _moe_ffn_kernel,
        grid_spec=grid_spec,
        out_shape=jax.ShapeDtypeStruct((n_slots, d), F32),
        compiler_params=_cparams(("arbitrary", "arbitrary"), vm),
        name="moe_expert_ffn",
    )(tile_expert, tile_src, tile_valid, xs, w1, w3, w2)


def _combine_ln_kernel(d1_ref, d2_ref, pk_ref, h_ref, y_ref, lg_ref, lb_ref, o_ref, ybuf_ref, sem, *, alpha):
    tb = h_ref.shape[0]

    def issue(t, carry):
        _row_copy(y_ref, d1_ref[0, 0, t], ybuf_ref.at[0], t, sem).start()
        _row_copy(y_ref, d2_ref[0, 0, t], ybuf_ref.at[1], t, sem).start()
        return carry

    def drain(t, carry):
        _row_copy(y_ref, 0, ybuf_ref.at[0], 0, sem).wait()
        _row_copy(y_ref, 0, ybuf_ref.at[1], 0, sem).wait()
        return carry

    lax.fori_loop(0, tb, issue, 0)
    lax.fori_loop(0, tb, drain, 0)
    pk = pk_ref[...]
    g1 = pk[:, _PK_G1:_PK_G1 + 1]
    g2 = pk[:, _PK_G2:_PK_G2 + 1]
    f = g1 * ybuf_ref[0] + g2 * ybuf_ref[1]
    o_ref[...] = _ln_rows(alpha * h_ref[...] + f, lg_ref[...], lb_ref[...])


def _combine_ln(h, pk, ys, d1, d2, lg, lb, alpha):
    m, d = h.shape
    tb = _tile(m, 256)
    nt = m // tb
    idx = pl.BlockSpec((1, 1, tb), lambda i: (i, 0, 0), memory_space=pltpu.SMEM)
    row = pl.BlockSpec((tb, d), lambda i: (i, 0))
    vec = pl.BlockSpec((1, d), lambda i: (0, 0))
    vm = 2 * (2 * _nbytes((tb, d), F32) + _nbytes((tb, pk.shape[1]), F32)) + 5 * _nbytes((tb, d), F32)
    return pl.pallas_call(
        functools.partial(_combine_ln_kernel, alpha=alpha),
        grid=(nt,),
        in_specs=[idx, idx, pl.BlockSpec((tb, pk.shape[1]), lambda i: (i, 0)), row,
                  pl.BlockSpec(memory_space=pl.ANY), vec, vec],
        out_specs=row,
        out_shape=jax.ShapeDtypeStruct((m, d), F32),
        scratch_shapes=[pltpu.VMEM((2, tb, d), F32), pltpu.SemaphoreType.DMA(())],
        compiler_params=_cparams(("arbitrary",), vm),
        name="moe_combine_ln",
    )(d1.reshape(nt, 1, tb), d2.reshape(nt, 1, tb), pk, h, ys, lg.reshape(1, d), lb.reshape(1, d))


def _rope_tables(n_tok):
    pos = jnp.arange(n_tok, dtype=F32)
    inv = ROPE_THETA ** (-jnp.arange(0, MLA_ROPE_DIM, 2, dtype=F32) / MLA_ROPE_DIM)
    ang = pos[:, None] * inv[None, :]
    cos, sin = jnp.cos(ang), jnp.sin(ang)
    zero = jnp.zeros_like(cos)
    return jnp.concatenate([cos, cos, zero, zero], -1), jnp.concatenate([-sin, sin, zero, zero], -1)


def _rel_bucket(rel):
    nb = N_BUCKETS // 2
    max_exact = nb // 2
    n = jnp.abs(rel)
    large = max_exact + (jnp.log(jnp.maximum(n, 1).astype(F32) / max_exact)
                         / math.log(MAX_DISTANCE / max_exact) * (nb - max_exact)).astype(jnp.int32)
    large = jnp.minimum(large, nb - 1)
    return jnp.where(rel > 0, nb, 0) + jnp.where(n < max_exact, n, large)


def _swa_index_tables(n_meta, n_real):
    nblk = n_real // BLOCK
    band = jnp.arange(3 * BLOCK)
    meta_pos = jnp.arange(n_meta)
    pad = BLOCK - n_meta

    def one(start, q_pos, first_block_only=False):
        r_key = start - BLOCK + band
        in_range = (r_key >= 0) & (r_key < n_real)
        if first_block_only:
            in_range = in_range & (r_key < BLOCK)
        k_pos = jnp.concatenate([n_meta + r_key, meta_pos, jnp.zeros((pad,), jnp.int32)])
        always = jnp.concatenate([jnp.zeros((3 * BLOCK,), bool), jnp.ones((n_meta,), bool), jnp.zeros((pad,), bool)])
        live = jnp.concatenate([in_range, jnp.ones((n_meta,), bool), jnp.zeros((pad,), bool)])
        rel = k_pos[None, :] - q_pos[:, None]
        vis = always[None, :] | (live[None, :] & (jnp.abs(rel) <= WINDOW))
        return _rel_bucket(rel), vis

    q_local = jnp.arange(BLOCK)
    tabs = [one(0, n_meta + q_local),
            one(BLOCK * min(1, nblk - 1), n_meta + BLOCK * min(1, nblk - 1) + q_local),
            one(BLOCK * (nblk - 1), n_meta + BLOCK * (nblk - 1) + q_local),
            one(0, jnp.where(q_local < n_meta, q_local, 0), first_block_only=True)]
    bucket = jnp.stack([t[0] for t in tabs]).astype(jnp.int32)
    vis = jnp.stack([t[1] for t in tabs]).astype(jnp.int32)
    return bucket, vis


def _layer_weights(l, w_in, w_uq, w_ukv, w_proj_a, w_proj_b, w_out, d_model, q_rank, kv_rank):
    hd = SWA_HEADS * SWA_HEAD_DIM
    kvd = SWA_KV_HEADS * SWA_HEAD_DIM
    splits = (q_rank, kv_rank, MLA_ROPE_DIM, hd, kvd, kvd, d_model, d_model)
    off = [0]
    for s in splits:
        off.append(off[-1] + s)
    col = lambda i: w_in[l][:, off[i]:off[i + 1]]
    order = (3, 6, 7, 1, 4, 5, 0)
    w_z = jnp.concatenate([col(i) for i in order], axis=1).astype(BF16)
    z_off = {}
    o = 0
    for i in order:
        z_off[i] = o
        o += splits[i]
    half = MLA_ROPE_DIM // 2
    kr = col(2)
    w_kr = jnp.concatenate([kr[:, :half], kr[:, half:], kr[:, :half], kr[:, half:]], axis=1).astype(BF16)
    uq = w_uq[l].reshape(q_rank, MLA_HEADS, MLA_NOPE_DIM + MLA_ROPE_DIM)
    uq_n = uq[:, :, :MLA_NOPE_DIM].reshape(q_rank, MLA_HEADS * MLA_NOPE_DIM).astype(BF16)
    r1 = uq[:, :, MLA_NOPE_DIM:MLA_NOPE_DIM + half]
    r2 = uq[:, :, MLA_NOPE_DIM + half:]
    uq_r = jnp.concatenate([r1, r2, r1, r2], axis=2).reshape(q_rank, MLA_HEADS * V7X_LANES).astype(BF16)
    ukv = w_ukv[l].reshape(kv_rank, MLA_HEADS, MLA_NOPE_DIM + MLA_V_DIM)
    uk = ukv[:, :, :MLA_NOPE_DIM].reshape(kv_rank, MLA_HEADS * MLA_NOPE_DIM).astype(BF16)
    uv = ukv[:, :, MLA_NOPE_DIM:].reshape(kv_rank, MLA_HEADS * MLA_V_DIM).astype(BF16)
    return dict(w_z=w_z, z_off=z_off, w_kr=w_kr, uq_n=uq_n, uq_r=uq_r, uk=uk, uv=uv,
                wa=w_proj_a[l].astype(BF16), wb=w_proj_b[l].astype(BF16), wo=w_out[l].astype(BF16))


def kernel(x, meta_tokens, emb_ln_g, emb_ln_b, rel_bias, w_in, q_norm_g, kv_norm_g, w_uq, w_ukv, sink_logits,
           w_proj_a, w_proj_b, w_out, ln_mix_g, ln_mix_b, ln_ffn_g, ln_ffn_b, ffn_w1, ffn_w3, ffn_w2, router_w,
           router_b, moe_w1, moe_w3, moe_w2):
    bsz, seq, d = x.shape
    n_meta = meta_tokens.shape[0]
    depth = w_in.shape[0]
    q_rank = q_norm_g.shape[1]
    kv_rank = kv_norm_g.shape[1]
    alpha = (2 * depth) ** 0.25
    nblk = seq // BLOCK
    assert seq % BLOCK == 0 and n_meta % 16 == 0 and n_meta <= BLOCK

    hr, hr_b = _embed_ln(x.reshape(bsz * seq, d), emb_ln_g, emb_ln_b)
    hm1, hm1_b = _embed_ln(meta_tokens.astype(x.dtype), emb_ln_g, emb_ln_b)
    hm, hm_b = jnp.tile(hm1, (bsz, 1)), jnp.tile(hm1_b, (bsz, 1))

    ct, st = _rope_tables(n_meta + seq)
    ct_r, st_r = ct[n_meta:], st[n_meta:]
    ct_m, st_m = jnp.tile(ct[:n_meta], (bsz, 1)), jnp.tile(st[:n_meta], (bsz, 1))
    bucket, vis = _swa_index_tables(n_meta, seq)
    bias_all = _bias_tables(rel_bias, bucket, vis)
    bias_real = bias_all[:3]
    bias_meta = bias_all[3:, :, :n_meta]

    last_blk = nblk - 1
    type_real = lambda i: jnp.where(i == 0, 0, jnp.where(i == last_blk, 2, 1))
    kv_real = lambda i: (jnp.maximum(i - 1, 0), i, jnp.minimum(i + 1, last_blk))
    type_meta = lambda i: 0
    kv_meta_q = lambda i: (0, 0, 0)

    for l in range(depth):
        last = l == depth - 1
        lw = _layer_weights(l, w_in, w_uq, w_ukv, w_proj_a, w_proj_b, w_out, d, q_rank, kv_rank)
        zo = lw["z_off"]
        sink = sink_logits[l]

        z_r = _matmul(hr_b, lw["w_z"], BF16)
        z_m = _matmul(hm_b, lw["w_z"], BF16)
        k_r, v_r = _kv_proj(hr_b, z_r, zo[1], kv_norm_g[l], lw["uk"], lw["uv"], lw["w_kr"], ct_r, st_r)
        k_m, v_m = _kv_proj(hm_b, z_m, zo[1], kv_norm_g[l], lw["uk"], lw["uv"], lw["w_kr"], ct_m, st_m)
        q_r = _q_proj(z_r, zo[0], q_norm_g[l], lw["uq_n"], lw["uq_r"], ct_r, st_r)
        oa_r = _mla_attn(q_r, k_r, k_m, v_r, v_m, bsz, n_meta)
        ob_r = _swa_attn(sink, z_r, BLOCK, z_r, z_m, zo[4], zo[5], bias_real, type_real, kv_real, bsz, nblk, n_meta)
        mg_r = _gate_proj(oa_r, ob_r, z_r, zo[6], zo[7], lw["wa"], lw["wb"])
        hr, hr_b = _out_ln(mg_r, lw["wo"], hr, ln_mix_g[l], ln_mix_b[l], alpha)
        if not last:
            q_m = _q_proj(z_m, zo[0], q_norm_g[l], lw["uq_n"], lw["uq_r"], ct_m, st_m)
            oa_m = _mla_attn(q_m, k_r, k_m, v_r, v_m, bsz, n_meta)
            ob_m = _swa_attn(sink, z_m, n_meta, z_r, z_m, zo[4], zo[5], bias_meta, type_meta, kv_meta_q, bsz, 1,
                             n_meta)
            mg_m = _gate_proj(oa_m, ob_m, z_m, zo[6], zo[7], lw["wa"], lw["wb"])
            hm, hm_b = _out_ln(mg_m, lw["wo"], hm, ln_mix_g[l], ln_mix_b[l], alpha)

        if l % 2 == 0:
            w1, w3, w2 = (ffn_w1[l // 2].astype(BF16), ffn_w3[l // 2].astype(BF16), ffn_w2[l // 2].astype(BF16))
            hr, hr_b = _ffn_down_ln(_ffn_up(hr_b, w1, w3), w2, hr, ln_ffn_g[l], ln_ffn_b[l], alpha)
            if not last:
                hm, hm_b = _ffn_down_ln(_ffn_up(hm_b, w1, w3), w2, hm, ln_ffn_g[l], ln_ffn_b[l], alpha)
        else:
            hr = _moe_layer(hr, router_w[l // 2], router_b[l // 2], moe_w1[l // 2], moe_w3[l // 2], moe_w2[l // 2],
                            ln_ffn_g[l], ln_ffn_b[l], alpha)
            hr_b = hr.astype(BF16)
            if not last:
                hm = _moe_layer(hm, router_w[l // 2], router_b[l // 2], moe_w1[l // 2], moe_w3[l // 2],
                                moe_w2[l // 2], ln_ffn_g[l], ln_ffn_b[l], alpha)
                hm_b = hm.astype(BF16)
    return hr.reshape(bsz, seq, d)


def _moe_layer(h, router_w, router_b, w1, w3, w2, lg, lb, alpha):
    m, d = h.shape
    n_exp = router_w.shape[1]
    tm = _tile(m, 512)
    w_pad = jnp.zeros((d, V7X_LANES), F32).at[:, :n_exp].set(router_w)
    b_pad = jnp.zeros((1, V7X_LANES), F32).at[0, :n_exp].set(router_b)
    pk, cnt = _router(h, w_pad, b_pad, n_exp)
    counts = cnt[0, :n_exp].astype(jnp.int32)
    padded = (counts + tm - 1) // tm * tm
    pad_end = jnp.cumsum(padded)
    pad_start = pad_end - padded
    e1 = pk[:, _PK_E1].astype(jnp.int32)
    e2 = pk[:, _PK_E2].astype(jnp.int32)
    d1 = pad_start[e1] + pk[:, _PK_R1].astype(jnp.int32)
    d2 = pad_start[e2] + pk[:, _PK_R2].astype(jnp.int32)
    n_tiles = (m * TOP_K) // tm + n_exp
    n_slots = n_tiles * tm
    n_used = pad_end[-1] // tm
    tile_id = jnp.arange(n_tiles, dtype=jnp.int32)
    tile_valid = (tile_id < n_used).astype(jnp.int32)
    tile_src = jnp.minimum(tile_id, n_used - 1).astype(jnp.int32)
    tile_expert = jnp.minimum(jnp.searchsorted(pad_end, tile_src * tm, side="right"), n_exp - 1).astype(jnp.int32)
    pad_info = jnp.stack([jnp.append(pad_start + counts, pad_end[-1]),
                          jnp.append(padded - counts, n_slots - pad_end[-1])]).astype(jnp.int32)

    xs = _dispatch(h, d1, d2, pad_info, n_slots)
    ys = _moe_ffn(xs, w1.astype(BF16), w3.astype(BF16), w2.astype(BF16), tile_expert, tile_src, tile_valid, tm)
    return _combine_ln(h, pk, ys, d1, d2, lg, lb, alpha)
```

```python
import functools
import math

import jax
import jax.numpy as jnp
from jax import lax
from jax.experimental import pallas as pl
from jax.experimental.pallas import tpu as pltpu

MLA_HEADS = 16
MLA_NOPE_DIM = 128
MLA_ROPE_DIM = 64
MLA_V_DIM = 128
SWA_HEADS = 16
SWA_KV_HEADS = 4
SWA_HEAD_DIM = 128
WINDOW = 128
BLOCK = 128
N_BUCKETS = 32
MAX_DISTANCE = 128
TOP_K = 2
ROPE_THETA = 10000.0
LN_EPS = 1e-5
RMS_EPS = 1e-6
NEG = -1e30
LOG2E = math.log2(math.e)

V7X_LANES = 128
V7X_MXU_DIM = 256
V7X_VMEM_LIMIT_BYTES = 56 * 1024 * 1024

F32 = jnp.float32
BF16 = jnp.bfloat16


COMPILER_SCRATCH_BYTES = 8 * 1024 * 1024


def _cparams(semantics, vmem_bytes):
    limit = int(min(V7X_VMEM_LIMIT_BYTES, vmem_bytes + COMPILER_SCRATCH_BYTES))
    return pltpu.CompilerParams(dimension_semantics=semantics, vmem_limit_bytes=limit)


def _nbytes(shape, dtype):
    return math.prod(shape) * jnp.dtype(dtype).itemsize


def _dot(a, b):
    return jnp.dot(a, b, preferred_element_type=F32)


def _dot_t(a, b):
    return lax.dot_general(a, b, (((1,), (1,)), ((), ())), preferred_element_type=F32)


def _tile(n, pref):
    t = min(n, pref)
    while n % t:
        t //= 2
    return t


def _ln_rows(y, g, b):
    mu = jnp.mean(y, axis=-1, keepdims=True)
    d = y - mu
    var = jnp.mean(d * d, axis=-1, keepdims=True)
    return d * lax.rsqrt(var + LN_EPS) * g + b


def _embed_ln_kernel(x_ref, g_ref, b_ref, h_ref, hb_ref):
    y = _ln_rows(x_ref[...], g_ref[...], b_ref[...])
    h_ref[...] = y
    hb_ref[...] = y.astype(BF16)


def _embed_ln(x, g, b):
    m, d = x.shape
    tm = _tile(m, 512)
    row = pl.BlockSpec((tm, d), lambda i: (i, 0))
    vec = pl.BlockSpec((1, d), lambda i: (0, 0))
    return pl.pallas_call(
        _embed_ln_kernel,
        grid=(m // tm,),
        in_specs=[row, vec, vec],
        out_specs=[row, row],
        out_shape=[jax.ShapeDtypeStruct((m, d), F32), jax.ShapeDtypeStruct((m, d), BF16)],
        compiler_params=_cparams(("parallel",), 2 * tm * d * 10 + 4 * tm * d * 4),
        name="embed_ln",
    )(x, g.reshape(1, d), b.reshape(1, d))


def _mm_kernel(a_ref, w_ref, o_ref):
    o_ref[...] = _dot(a_ref[...], w_ref[...]).astype(o_ref.dtype)


def _matmul(a, w, out_dtype, tm_pref=1024, tn_pref=768):
    m, k = a.shape
    n = w.shape[1]
    tm = _tile(m, tm_pref)
    tn = next(t for t in (tn_pref, 512, 256, 128) if n % t == 0)
    vm = 2 * (_nbytes((tm, k), BF16) + _nbytes((k, tn), BF16) + _nbytes((tm, tn), out_dtype)) + _nbytes((tm, tn), F32)
    return pl.pallas_call(
        _mm_kernel,
        grid=(m // tm, n // tn),
        in_specs=[pl.BlockSpec((tm, k), lambda i, j: (i, 0)), pl.BlockSpec((k, tn), lambda i, j: (0, j))],
        out_specs=pl.BlockSpec((tm, tn), lambda i, j: (i, j)),
        out_shape=jax.ShapeDtypeStruct((m, n), out_dtype),
        compiler_params=_cparams(("parallel", "parallel"), vm),
        name="in_proj",
    )(a, w)


def _rope(x, ct, st):
    return x * ct + pltpu.roll(x, MLA_ROPE_DIM // 2, 1) * st


def _rms(c, g):
    return c * lax.rsqrt(jnp.mean(c * c, axis=-1, keepdims=True) + RMS_EPS) * g


def _q_proj_kernel(c_ref, g_ref, wn_ref, wr_ref, ct_ref, st_ref, o_ref, *, heads, scale):
    cb = _rms(c_ref[...].astype(F32), g_ref[...]).astype(BF16)
    qn = _dot(cb, wn_ref[...])
    qr = _dot(cb, wr_ref[...])
    ct = ct_ref[...]
    st = st_ref[...]
    w = MLA_NOPE_DIM + V7X_LANES
    for h in range(heads):
        o_ref[:, h * w:h * w + MLA_NOPE_DIM] = (qn[:, h * MLA_NOPE_DIM:(h + 1) * MLA_NOPE_DIM] * scale).astype(BF16)
        xr = qr[:, h * V7X_LANES:(h + 1) * V7X_LANES]
        o_ref[:, h * w + MLA_NOPE_DIM:(h + 1) * w] = (_rope(xr, ct, st) * scale).astype(BF16)


def _q_proj(z, cq_off, g, wn, wr, ct, st):
    m = z.shape[0]
    r = g.shape[0]
    heads = MLA_HEADS
    tm = _tile(math.gcd(m, ct.shape[0]), 512)
    npos = ct.shape[0] // tm
    assert cq_off % r == 0
    wout = heads * (MLA_NOPE_DIM + V7X_LANES)
    scale = (MLA_NOPE_DIM + MLA_ROPE_DIM) ** -0.5 * LOG2E
    vm = (2 * (_nbytes((tm, r), BF16) + _nbytes(wn.shape, BF16) + _nbytes(wr.shape, BF16) + 2 * tm * V7X_LANES * 4
               + _nbytes((tm, wout), BF16)) + 2 * _nbytes((tm, wn.shape[1]), F32) + _nbytes((tm, r), F32) * 2)
    tab = pl.BlockSpec((tm, V7X_LANES), lambda i: (i % npos, 0))
    return pl.pallas_call(
        functools.partial(_q_proj_kernel, heads=heads, scale=scale),
        grid=(m // tm,),
        in_specs=[pl.BlockSpec((tm, r), lambda i: (i, cq_off // r)),
                  pl.BlockSpec((1, r), lambda i: (0, 0)),
                  pl.BlockSpec(wn.shape, lambda i: (0, 0)),
                  pl.BlockSpec(wr.shape, lambda i: (0, 0)),
                  tab, tab],
        out_specs=pl.BlockSpec((tm, wout), lambda i: (i, 0)),
        out_shape=jax.ShapeDtypeStruct((m, wout), BF16),
        compiler_params=_cparams(("parallel",), vm),
        name="mla_q_proj",
    )(z, g.reshape(1, r), wn, wr, ct, st)


def _kv_proj_kernel(hb_ref, c_ref, g_ref, wk_ref, wv_ref, wkr_ref, ct_ref, st_ref, k_ref, v_ref, *, heads,
                    transpose_v):
    cb = _rms(c_ref[...].astype(F32), g_ref[...]).astype(BF16)
    kn = _dot(cb, wk_ref[...])
    if transpose_v:
        v_ref[...] = _dot_t(wv_ref[...], cb).astype(BF16)
    else:
        v_ref[...] = _dot(cb, wv_ref[...]).astype(BF16)
    kr = _rope(_dot(hb_ref[...], wkr_ref[...]), ct_ref[...], st_ref[...]).astype(BF16)
    w = MLA_NOPE_DIM + V7X_LANES
    for h in range(heads):
        k_ref[:, h * w:h * w + MLA_NOPE_DIM] = kn[:, h * MLA_NOPE_DIM:(h + 1) * MLA_NOPE_DIM].astype(BF16)
        k_ref[:, h * w + MLA_NOPE_DIM:(h + 1) * w] = kr


def _kv_proj(hb, z, ckv_off, g, wk, wv, wkr, ct, st, transpose_v):
    m, d = hb.shape
    r = g.shape[0]
    heads = MLA_HEADS
    tm = _tile(math.gcd(m, ct.shape[0]), 512)
    npos = ct.shape[0] // tm
    assert ckv_off % r == 0
    wk_out = heads * (MLA_NOPE_DIM + V7X_LANES)
    wv_out = heads * MLA_V_DIM
    if transpose_v:
        v_spec = pl.BlockSpec((wv_out, tm), lambda i: (0, i))
        v_shape = jax.ShapeDtypeStruct((wv_out, m), BF16)
    else:
        v_spec = pl.BlockSpec((tm, wv_out), lambda i: (i, 0))
        v_shape = jax.ShapeDtypeStruct((m, wv_out), BF16)
    vm = (2 * (_nbytes((tm, d), BF16) + _nbytes((tm, r), BF16) + _nbytes(wk.shape, BF16) + _nbytes(wv.shape, BF16)
               + _nbytes(wkr.shape, BF16) + 2 * tm * V7X_LANES * 4 + _nbytes((tm, wk_out), BF16)
               + _nbytes((tm, wv_out), BF16)) + 2 * _nbytes((tm, wv_out), F32) + _nbytes((tm, r), F32) * 2)
    tab = pl.BlockSpec((tm, V7X_LANES), lambda i: (i % npos, 0))
    return pl.pallas_call(
        functools.partial(_kv_proj_kernel, heads=heads, transpose_v=transpose_v),
        grid=(m // tm,),
        in_specs=[pl.BlockSpec((tm, d), lambda i: (i, 0)),
                  pl.BlockSpec((tm, r), lambda i: (i, ckv_off // r)),
                  pl.BlockSpec((1, r), lambda i: (0, 0)),
                  pl.BlockSpec(wk.shape, lambda i: (0, 0)),
                  pl.BlockSpec(wv.shape, lambda i: (0, 0)),
                  pl.BlockSpec(wkr.shape, lambda i: (0, 0)),
                  tab, tab],
        out_specs=[pl.BlockSpec((tm, wk_out), lambda i: (i, 0)), v_spec],
        out_shape=[jax.ShapeDtypeStruct((m, wk_out), BF16), v_shape],
        compiler_params=_cparams(("parallel",), vm),
        name="mla_kv_proj",
    )(hb, z, g.reshape(1, r), wk, wv, wkr, ct, st)


def _with_ones(v):
    return jnp.concatenate([v, jnp.ones(v.shape, v.dtype)], axis=1)


ONES_ROWS = 16


def _mla_attn_kernel(q_ref, k_ref, km_ref, vt_ref, vm_ref, o_ref, s_ref, m_ref, acc_ref, *, tk, n_meta, hb,
                     unroll):
    wq = MLA_NOPE_DIM + V7X_LANES
    dv = MLA_V_DIM
    tq = s_ref.shape[2]
    n_chunks = k_ref.shape[0] // tk
    pad = V7X_LANES - n_meta
    assert n_chunks % unroll == 0

    def q_of(j):
        q = q_ref[:, j * wq:(j + 1) * wq]
        if q.shape[0] < tq:
            q = jnp.concatenate([q, jnp.zeros((tq - q.shape[0], wq), BF16)], axis=0)
        return q

    def scores(j, c):
        start = pl.multiple_of(c * tk, tk)
        return _dot_t(k_ref[pl.ds(start, tk), j * wq:(j + 1) * wq], q_of(j))

    def accumulate(j, c, s):
        start = pl.multiple_of(c * tk, tk)
        m = m_ref[j, 0:1]
        m_new = jnp.maximum(m, jnp.max(s, axis=0, keepdims=True))
        vt = jnp.concatenate([vt_ref[j * dv:(j + 1) * dv, pl.ds(start, tk)], jnp.ones((ONES_ROWS, tk), BF16)], axis=0)
        acc_ref[j] = jnp.exp2(m - m_new) * acc_ref[j] + _dot(vt, jnp.exp2(s - m_new).astype(BF16))
        m_ref[j, 0:1] = m_new

    for j in range(hb):
        kmp = jnp.concatenate([km_ref[:, j * wq:(j + 1) * wq], jnp.zeros((pad, wq), BF16)], axis=0)
        vmp = jnp.concatenate([vm_ref[:, j * dv:(j + 1) * dv].astype(F32), jnp.zeros((pad, dv), F32)], axis=0)
        vt = jnp.concatenate([vmp.T.astype(BF16), jnp.ones((ONES_ROWS, V7X_LANES), BF16)], axis=0)
        s = _dot_t(kmp, q_of(j))
        row = lax.broadcasted_iota(jnp.int32, s.shape, 0)
        s = jnp.where(row < n_meta, s, NEG)
        m0 = jnp.max(s, axis=0, keepdims=True)
        m_ref[j, 0:1] = m0
        acc_ref[j] = _dot(vt, jnp.exp2(s - m0).astype(BF16))
        s_ref[j] = scores(j, 0)

    def group(c0, prefetch):
        cur = [s_ref[j] for j in range(hb)]
        for u in range(unroll):
            nxt = None
            if u < unroll - 1:
                nxt = [scores(j, c0 + u + 1) for j in range(hb)]
            elif prefetch:
                for j in range(hb):
                    s_ref[j] = scores(j, c0 + unroll)
            for j in range(hb):
                accumulate(j, c0 + u, cur[j])
            cur = nxt

    def body(i, carry):
        group(unroll * i, True)
        return carry

    lax.fori_loop(0, n_chunks // unroll - 1, body, 0)
    group(n_chunks - unroll, False)
    for j in range(hb):
        acc = acc_ref[j]
        o = (acc[:dv] / acc[dv:dv + 1]).T
        o_ref[:, j * dv:(j + 1) * dv] = o[:o_ref.shape[0]].astype(o_ref.dtype)


def _mla_attn(q, k, km, vt, vmeta, batch, n_meta):
    heads = MLA_HEADS
    hb = 2
    wq = MLA_NOPE_DIM + V7X_LANES
    dv = MLA_V_DIM
    rows_q = q.shape[0] // batch
    seq = k.shape[0] // batch
    tq = _tile(rows_q, 512)
    tk = _tile(seq, V7X_MXU_DIM)
    unroll = _tile(seq // tk, 4)
    nq = rows_q // tq
    tqp = max(tq, V7X_LANES)
    vm = (2 * hb * (_nbytes((tq, wq), BF16) + _nbytes((seq, wq), BF16) + _nbytes((seq, dv), BF16)
                    + _nbytes((tq, dv), BF16)) + hb * (6 * _nbytes((tk, tqp), F32) + 2 * _nbytes((2 * dv, tqp), F32)))
    return pl.pallas_call(
        functools.partial(_mla_attn_kernel, tk=tk, n_meta=n_meta, hb=hb, unroll=unroll),
        grid=(batch, heads // hb, nq),
        in_specs=[pl.BlockSpec((tq, hb * wq), lambda b, h, i: (b * nq + i, h)),
                  pl.BlockSpec((seq, hb * wq), lambda b, h, i: (b, h)),
                  pl.BlockSpec((n_meta, hb * wq), lambda b, h, i: (b, h)),
                  pl.BlockSpec((hb * dv, seq), lambda b, h, i: (h, b)),
                  pl.BlockSpec((n_meta, hb * dv), lambda b, h, i: (b, h))],
        out_specs=pl.BlockSpec((tq, hb * dv), lambda b, h, i: (b * nq + i, h)),
        out_shape=jax.ShapeDtypeStruct((q.shape[0], heads * dv), BF16),
        scratch_shapes=[pltpu.VMEM((hb, tk, tqp), F32), pltpu.VMEM((hb, 8, tqp), F32),
                        pltpu.VMEM((hb, dv + ONES_ROWS, tqp), F32)],
        compiler_params=_cparams(("parallel", "parallel", "parallel"), vm),
        name="mla_attn",
    )(q, k, km, vt, vmeta)


def _bias_kernel(rb_ref, bk_ref, vis_ref, o_ref, *, n_buckets):
    h = pl.program_id(1)
    bk = bk_ref[0]
    acc = jnp.zeros(bk.shape, F32)
    for j in range(n_buckets):
        acc = jnp.where(bk == j, rb_ref[j, h] * LOG2E, acc)
    o_ref[0, 0] = jnp.where(vis_ref[0] != 0, acc, NEG)


def _bias_tables(rel_bias, bucket, vis):
    t, r, s = bucket.shape
    heads = rel_bias.shape[1]
    blk = pl.BlockSpec((1, r, s), lambda i, h: (i, 0, 0))
    return pl.pallas_call(
        functools.partial(_bias_kernel, n_buckets=rel_bias.shape[0]),
        grid=(t, heads),
        in_specs=[pl.BlockSpec(memory_space=pltpu.SMEM), blk, blk],
        out_specs=pl.BlockSpec((1, 1, r, s), lambda i, h: (i, h, 0, 0)),
        out_shape=jax.ShapeDtypeStruct((t, heads, r, s), F32),
        compiler_params=_cparams(("parallel", "parallel"), 16 * r * s * 4),
        name="swa_bias_table",
    )(rel_bias, bucket, vis)


def _swa_kernel(sink_ref, q_ref, kp_ref, ko_ref, kn_ref, km_ref, vp_ref, vo_ref, vn_ref, vm_ref, bias_ref, o_ref,
                *, kv_heads, group, scale, n_meta):
    d = SWA_HEAD_DIM
    r = q_ref.shape[0]
    zpad = jnp.zeros((BLOCK - n_meta, d), BF16)
    for g in range(kv_heads):
        cols = slice(g * d, (g + 1) * d)
        kcat = jnp.concatenate([kp_ref[:, cols], ko_ref[:, cols], kn_ref[:, cols], km_ref[:, cols], zpad], axis=0)
        vcat = jnp.concatenate([vp_ref[:, cols], vo_ref[:, cols], vn_ref[:, cols], vm_ref[:, cols], zpad], axis=0)
        heads = range(g * group, (g + 1) * group)
        q = jnp.concatenate([q_ref[:, h * d:(h + 1) * d] for h in heads], axis=0)
        bias = jnp.concatenate([bias_ref[0, h] for h in heads], axis=0)
        sink = jnp.concatenate([jnp.full((r, 1), sink_ref[h] * LOG2E, F32) for h in heads], axis=0)
        s = _dot_t(q, kcat) * scale + bias
        m = jnp.maximum(jnp.max(s, axis=-1, keepdims=True), sink)
        acc = _dot(jnp.exp2(s - m).astype(BF16), _with_ones(vcat))
        o = acc[:, :d] / (acc[:, d:] + jnp.exp2(sink - m))
        for j, h in enumerate(heads):
            o_ref[:, h * d:(h + 1) * d] = o[j * r:(j + 1) * r].astype(o_ref.dtype)


def _swa_attn(sink, q_src, q_rows, kv_src, kv_meta, k_off, v_off, bias, type_of_block, kv_block_of, batch, n_grid_blk,
              n_meta):
    d = SWA_HEAD_DIM
    group = SWA_HEADS // SWA_KV_HEADS
    wq, wkv = SWA_HEADS * d, SWA_KV_HEADS * d
    nblk_kv = kv_src.shape[0] // batch // BLOCK
    s_keys = bias.shape[-1]
    assert k_off % wkv == 0 and v_off % wkv == 0
    kc, vc = k_off // wkv, v_off // wkv

    def kv_spec(which, col):
        return pl.BlockSpec((BLOCK, wkv), lambda b, i: (b * nblk_kv + kv_block_of(i)[which], col))

    def meta_spec(col):
        return pl.BlockSpec((n_meta, wkv), lambda b, i: (b, col))

    rows = group * q_rows
    vm = (2 * (2 * _nbytes((q_rows, wq), BF16) + 8 * _nbytes((BLOCK, wkv), BF16)
               + _nbytes((SWA_HEADS, q_rows, s_keys), F32))
          + SWA_KV_HEADS * (4 * _nbytes((rows, s_keys), F32) + _nbytes((rows, 2 * d), F32)))
    return pl.pallas_call(
        functools.partial(_swa_kernel, kv_heads=SWA_KV_HEADS, group=group, scale=d ** -0.5 * LOG2E, n_meta=n_meta),
        grid=(batch, n_grid_blk),
        in_specs=[pl.BlockSpec(memory_space=pltpu.SMEM),
                  pl.BlockSpec((q_rows, wq), lambda b, i: (b * n_grid_blk + i, 0)),
                  kv_spec(0, kc), kv_spec(1, kc), kv_spec(2, kc), meta_spec(kc),
                  kv_spec(0, vc), kv_spec(1, vc), kv_spec(2, vc), meta_spec(vc),
                  pl.BlockSpec((1, SWA_HEADS, q_rows, s_keys), lambda b, i: (type_of_block(i), 0, 0, 0))],
        out_specs=pl.BlockSpec((q_rows, wq), lambda b, i: (b * n_grid_blk + i, 0)),
        out_shape=jax.ShapeDtypeStruct((batch * n_grid_blk * q_rows, wq), BF16),
        compiler_params=_cparams(("parallel", "parallel"), vm),
        name="swa_attn",
    )(sink, q_src, kv_src, kv_src, kv_src, kv_meta, kv_src, kv_src, kv_src, kv_meta, bias)


def _gate_proj_kernel(oa_ref, ob_ref, ga_ref, gb_ref, wa_ref, wb_ref, o_ref):
    pa = _dot(oa_ref[...], wa_ref[...])
    pb = _dot(ob_ref[...], wb_ref[...])
    ga = jax.nn.sigmoid(ga_ref[...].astype(F32))
    gb = jax.nn.sigmoid(gb_ref[...].astype(F32))
    o_ref[...] = (ga * pa + gb * pb).astype(o_ref.dtype)


def _gate_proj(oa, ob, z, ga_off, gb_off, wa, wb):
    m, ka = oa.shape
    kb = ob.shape[1]
    n = wa.shape[1]
    tm = _tile(m, 1024)
    tn = _tile(n, 512)
    assert ga_off % tn == 0 and gb_off % tn == 0
    vm = (2 * (_nbytes((tm, ka + kb), BF16) + _nbytes((ka + kb, tn), BF16) + 3 * _nbytes((tm, tn), BF16))
          + 4 * _nbytes((tm, tn), F32))
    act = lambda k: pl.BlockSpec((tm, k), lambda i, j: (i, 0))
    wsp = lambda k: pl.BlockSpec((k, tn), lambda i, j: (0, j))
    return pl.pallas_call(
        _gate_proj_kernel,
        grid=(m // tm, n // tn),
        in_specs=[act(ka), act(kb),
                  pl.BlockSpec((tm, tn), lambda i, j: (i, ga_off // tn + j)),
                  pl.BlockSpec((tm, tn), lambda i, j: (i, gb_off // tn + j)),
                  wsp(ka), wsp(kb)],
        out_specs=pl.BlockSpec((tm, tn), lambda i, j: (i, j)),
        out_shape=jax.ShapeDtypeStruct((m, n), BF16),
        compiler_params=_cparams(("parallel", "parallel"), vm),
        name="gate_proj",
    )(oa, ob, z, z, wa, wb)


def _out_ln_kernel(m_ref, w_ref, h_ref, g_ref, b_ref, ho_ref, hbo_ref, *, alpha):
    y = alpha * h_ref[...] + _dot(m_ref[...], w_ref[...])
    y = _ln_rows(y, g_ref[...], b_ref[...])
    ho_ref[...] = y
    hbo_ref[...] = y.astype(BF16)


def _out_ln(merged, w, h, g, b, alpha):
    m, d = h.shape
    tm = _tile(m, 256)
    row = lambda dt: pl.BlockSpec((tm, d), lambda i: (i, 0))
    vec = pl.BlockSpec((1, d), lambda i: (0, 0))
    vm = 2 * (_nbytes((tm, d), BF16) * 2 + _nbytes(w.shape, BF16) + 2 * _nbytes((tm, d), F32)) + 3 * _nbytes((tm, d), F32)
    return pl.pallas_call(
        functools.partial(_out_ln_kernel, alpha=alpha),
        grid=(m // tm,),
        in_specs=[row(BF16), pl.BlockSpec(w.shape, lambda i: (0, 0)), row(F32), vec, vec],
        out_specs=[row(F32), row(BF16)],
        out_shape=[jax.ShapeDtypeStruct((m, d), F32), jax.ShapeDtypeStruct((m, d), BF16)],
        compiler_params=_cparams(("parallel",), vm),
        name="out_proj_ln",
    )(merged, w, h, g.reshape(1, d), b.reshape(1, d))


def _swiglu(a1, a3):
    return a1 * jax.nn.sigmoid(a1) * a3


def _ffn_up_kernel(h_ref, w1_ref, w3_ref, o_ref):
    h = h_ref[...]
    o_ref[...] = _swiglu(_dot(h, w1_ref[...]), _dot(h, w3_ref[...])).astype(o_ref.dtype)


def _ffn_up(hb, w1, w3):
    m, d = hb.shape
    f = w1.shape[1]
    tm = _tile(m, 1024)
    tf = _tile(f, 512)
    vm = 2 * (_nbytes((tm, d), BF16) + 2 * _nbytes((d, tf), BF16) + _nbytes((tm, tf), BF16)) + 4 * _nbytes((tm, tf), F32)
    wsp = pl.BlockSpec((d, tf), lambda i, j: (0, j))
    return pl.pallas_call(
        _ffn_up_kernel,
        grid=(m // tm, f // tf),
        in_specs=[pl.BlockSpec((tm, d), lambda i, j: (i, 0)), wsp, wsp],
        out_specs=pl.BlockSpec((tm, tf), lambda i, j: (i, j)),
        out_shape=jax.ShapeDtypeStruct((m, f), BF16),
        compiler_params=_cparams(("parallel", "parallel"), vm),
        name="ffn_up",
    )(hb, w1, w3)


def _ffn_down_ln_kernel(g_ref, w2_ref, h_ref, lg_ref, lb_ref, ho_ref, hbo_ref, acc_ref, *, alpha):
    k = pl.program_id(1)

    @pl.when(k == 0)
    def _():
        acc_ref[...] = jnp.zeros_like(acc_ref)

    acc_ref[...] += _dot(g_ref[...], w2_ref[...])

    @pl.when(k == pl.num_programs(1) - 1)
    def _():
        y = _ln_rows(alpha * h_ref[...] + acc_ref[...], lg_ref[...], lb_ref[...])
        ho_ref[...] = y
        hbo_ref[...] = y.astype(BF16)


def _ffn_down_ln(gact, w2, h, lg, lb, alpha):
    m, d = h.shape
    f = w2.shape[0]
    tm = _tile(m, 512)
    tk = _tile(f, 512)
    vm = (2 * (_nbytes((tm, tk), BF16) + _nbytes((tk, d), BF16) + 2 * _nbytes((tm, d), F32) + _nbytes((tm, d), BF16))
          + 3 * _nbytes((tm, d), F32))
    row = pl.BlockSpec((tm, d), lambda i, k: (i, 0))
    vec = pl.BlockSpec((1, d), lambda i, k: (0, 0))
    return pl.pallas_call(
        functools.partial(_ffn_down_ln_kernel, alpha=alpha),
        grid=(m // tm, f // tk),
        in_specs=[pl.BlockSpec((tm, tk), lambda i, k: (i, k)), pl.BlockSpec((tk, d), lambda i, k: (k, 0)), row, vec, vec],
        out_specs=[row, row],
        out_shape=[jax.ShapeDtypeStruct((m, d), F32), jax.ShapeDtypeStruct((m, d), BF16)],
        scratch_shapes=[pltpu.VMEM((tm, d), F32)],
        compiler_params=_cparams(("parallel", "arbitrary"), vm),
        name="ffn_down_ln",
    )(gact, w2, h, lg.reshape(1, d), lb.reshape(1, d))


_PK_E1, _PK_E2, _PK_G1, _PK_G2, _PK_R1, _PK_R2 = range(6)


def _router_kernel(h_ref, w_ref, b_ref, pk_ref, cnt_ref, carry_ref, *, n_exp):
    @pl.when(pl.program_id(0) == 0)
    def _():
        carry_ref[...] = jnp.zeros_like(carry_ref)

    logits = jnp.dot(h_ref[...], w_ref[...], preferred_element_type=F32, precision=lax.Precision.HIGHEST) + b_ref[...]
    tm, nl = logits.shape
    lane = lax.broadcasted_iota(jnp.int32, (tm, nl), 1).astype(F32)
    s = jnp.where(lane < n_exp, logits, -jnp.inf)
    m1 = jnp.max(s, axis=-1, keepdims=True)
    i1 = jnp.min(jnp.where(s == m1, lane, float(nl)), axis=-1, keepdims=True)
    s2 = jnp.where(lane == i1, -jnp.inf, s)
    m2 = jnp.max(s2, axis=-1, keepdims=True)
    i2 = jnp.min(jnp.where(s2 == m2, lane, float(nl)), axis=-1, keepdims=True)
    e = jnp.exp(m2 - m1)
    g1 = 1.0 / (1.0 + e)
    g2 = e / (1.0 + e)
    oh1 = lane == i1
    oh2 = lane == i2
    both = jnp.where(oh1 | oh2, 1.0, 0.0)
    r = lax.broadcasted_iota(jnp.int32, (tm, tm), 0)
    c = lax.broadcasted_iota(jnp.int32, (tm, tm), 1)
    tri = jnp.where(c < r, 1.0, 0.0).astype(BF16)
    prefix = _dot(tri, both.astype(BF16)) + carry_ref[...]
    r1 = jnp.sum(jnp.where(oh1, prefix, 0.0), axis=-1, keepdims=True)
    r2 = jnp.sum(jnp.where(oh2, prefix, 0.0), axis=-1, keepdims=True)
    carry_ref[...] += jnp.sum(both, axis=0, keepdims=True)
    cnt_ref[...] = carry_ref[...]
    pk = jnp.zeros((tm, nl), F32)
    for col, val in ((_PK_E1, i1), (_PK_E2, i2), (_PK_G1, g1), (_PK_G2, g2), (_PK_R1, r1), (_PK_R2, r2)):
        pk = jnp.where(lane == col, val, pk)
    pk_ref[...] = pk


def _router(h, w_pad, b_pad, n_exp):
    m, d = h.shape
    nl = w_pad.shape[1]
    tm = _tile(m, 512)
    vm = 2 * (_nbytes((tm, d), F32) + _nbytes((d, nl), F32) + _nbytes((tm, nl), F32)) + _nbytes((tm, tm), F32) * 3
    return pl.pallas_call(
        functools.partial(_router_kernel, n_exp=n_exp),
        grid=(m // tm,),
        in_specs=[pl.BlockSpec((tm, d), lambda i: (i, 0)), pl.BlockSpec((d, nl), lambda i: (0, 0)),
                  pl.BlockSpec((1, nl), lambda i: (0, 0))],
        out_specs=[pl.BlockSpec((tm, nl), lambda i: (i, 0)), pl.BlockSpec((1, nl), lambda i: (0, 0))],
        out_shape=[jax.ShapeDtypeStruct((m, nl), F32), jax.ShapeDtypeStruct((1, nl), F32)],
        scratch_shapes=[pltpu.VMEM((1, nl), F32)],
        compiler_params=_cparams(("arbitrary",), vm),
        name="moe_router",
    )(h, w_pad, b_pad)


def _row_copy(src_ref, src_row, dst_ref, dst_row, sem):
    return pltpu.make_async_copy(src_ref.at[pl.ds(src_row, 1)], dst_ref.at[pl.ds(dst_row, 1)], sem)


def _dispatch_kernel(d1_ref, d2_ref, pad_ref, h_ref, zero_ref, xs_ref, sem, *, n_ranges):
    tb = h_ref.shape[0]

    def issue(t, carry):
        _row_copy(h_ref, t, xs_ref, d1_ref[0, 0, t], sem).start()
        _row_copy(h_ref, t, xs_ref, d2_ref[0, 0, t], sem).start()
        return carry

    def drain(t, carry):
        _row_copy(h_ref, 0, xs_ref, 0, sem).wait()
        _row_copy(h_ref, 0, xs_ref, 0, sem).wait()
        return carry

    lax.fori_loop(0, tb, issue, 0)

    @pl.when(pl.program_id(0) == 0)
    def _():
        for e in range(n_ranges):
            base = pad_ref[0, e]
            n = pad_ref[1, e]
            lax.fori_loop(0, n, lambda j, c: (_row_copy(zero_ref, 0, xs_ref, base + j, sem).start(), c)[1], 0)
            lax.fori_loop(0, n, lambda j, c: (_row_copy(zero_ref, 0, xs_ref, 0, sem).wait(), c)[1], 0)

    lax.fori_loop(0, tb, drain, 0)


def _dispatch(h, d1, d2, pad_info, n_slots):
    m, d = h.shape
    tb = _tile(m, 256)
    nt = m // tb
    idx = pl.BlockSpec((1, 1, tb), lambda i: (i, 0, 0), memory_space=pltpu.SMEM)
    zero = jnp.zeros((8, d), F32)
    return pl.pallas_call(
        functools.partial(_dispatch_kernel, n_ranges=pad_info.shape[1]),
        grid=(nt,),
        in_specs=[idx, idx, pl.BlockSpec(memory_space=pltpu.SMEM), pl.BlockSpec((tb, d), lambda i: (i, 0)),
                  pl.BlockSpec((8, d), lambda i: (0, 0))],
        out_specs=pl.BlockSpec(memory_space=pl.ANY),
        out_shape=jax.ShapeDtypeStruct((n_slots, d), F32),
        scratch_shapes=[pltpu.SemaphoreType.DMA(())],
        compiler_params=_cparams(("arbitrary",), 4 * tb * d * 4),
        name="moe_dispatch",
    )(d1.reshape(nt, 1, tb), d2.reshape(nt, 1, tb), pad_info, h, zero)


def _moe_ffn_kernel(te_ref, ts_ref, tv_ref, x_ref, w1_ref, w3_ref, w2_ref, y_ref, xb_ref):
    i = pl.program_id(0)
    j = pl.program_id(1)

    @pl.when(tv_ref[i] != 0)
    def _():
        @pl.when(j == 0)
        def _():
            xb_ref[...] = x_ref[...].astype(BF16)
            y_ref[...] = jnp.zeros_like(y_ref)

        xb = xb_ref[...]
        gact = _swiglu(_dot(xb, w1_ref[0]), _dot(xb, w3_ref[0])).astype(BF16)
        y_ref[...] += _dot(gact, w2_ref[0])

    @pl.when((tv_ref[i] == 0) & (j == 0))
    def _():
        y_ref[...] = jnp.zeros_like(y_ref)


def _moe_ffn(xs, w1, w3, w2, tile_expert, tile_src, tile_valid, tm):
    n_slots, d = xs.shape
    f = w1.shape[2]
    tf = _tile(f, 512)
    nf = f // tf
    n_tiles = n_slots // tm

    def jj(i, j, tv):
        return jnp.where(tv[i] != 0, j, nf - 1)

    vm = (2 * (2 * _nbytes((tm, d), F32) + 3 * _nbytes((d, tf), BF16)) + _nbytes((tm, d), BF16)
          + 4 * _nbytes((tm, tf), F32) + _nbytes((tm, d), F32))
    grid_spec = pltpu.PrefetchScalarGridSpec(
        num_scalar_prefetch=3,
        grid=(n_tiles, nf),
        in_specs=[pl.BlockSpec((tm, d), lambda i, j, te, ts, tv: (ts[i], 0)),
                  pl.BlockSpec((1, d, tf), lambda i, j, te, ts, tv: (te[i], 0, jj(i, j, tv))),
                  pl.BlockSpec((1, d, tf), lambda i, j, te, ts, tv: (te[i], 0, jj(i, j, tv))),
                  pl.BlockSpec((1, tf, d), lambda i, j, te, ts, tv: (te[i], jj(i, j, tv), 0))],
        out_specs=pl.BlockSpec((tm, d), lambda i, j, te, ts, tv: (i, 0)),
        scratch_shapes=[pltpu.VMEM((tm, d), BF16)],
    )
    return pl.pallas_call(
        _moe_ffn_kernel,
        grid_spec=grid_spec,
        out_shape=jax.ShapeDtypeStruct((n_slots, d), F32),
        compiler_params=_cparams(("arbitrary", "arbitrary"), vm),
        name="moe_expert_ffn",
    )(tile_expert, tile_src, tile_valid, xs, w1, w3, w2)


def _combine_ln_kernel(d1_ref, d2_ref, pk_ref, h_ref, y_ref, lg_ref, lb_ref, o_ref, ybuf_ref, sem, *, alpha):
    tb = h_ref.shape[0]

    def issue(t, carry):
        _row_copy(y_ref, d1_ref[0, 0, t], ybuf_ref.at[0], t, sem).start()
        _row_copy(y_ref, d2_ref[0, 0, t], ybuf_ref.at[1], t, sem).start()
        return carry

    def drain(t, carry):
        _row_copy(y_ref, 0, ybuf_ref.at[0], 0, sem).wait()
        _row_copy(y_ref, 0, ybuf_ref.at[1], 0, sem).wait()
        return carry

    lax.fori_loop(0, tb, issue, 0)
    lax.fori_loop(0, tb, drain, 0)
    pk = pk_ref[...]
    g1 = pk[:, _PK_G1:_PK_G1 + 1]
    g2 = pk[:, _PK_G2:_PK_G2 + 1]
    f = g1 * ybuf_ref[0] + g2 * ybuf_ref[1]
    o_ref[...] = _ln_rows(alpha * h_ref[...] + f, lg_ref[...], lb_ref[...])


def _combine_ln(h, pk, ys, d1, d2, lg, lb, alpha):
    m, d = h.shape
    tb = _tile(m, 256)
    nt = m // tb
    idx = pl.BlockSpec((1, 1, tb), lambda i: (i, 0, 0), memory_space=pltpu.SMEM)
    row = pl.BlockSpec((tb, d), lambda i: (i, 0))
    vec = pl.BlockSpec((1, d), lambda i: (0, 0))
    vm = 2 * (2 * _nbytes((tb, d), F32) + _nbytes((tb, pk.shape[1]), F32)) + 5 * _nbytes((tb, d), F32)
    return pl.pallas_call(
        functools.partial(_combine_ln_kernel, alpha=alpha),
        grid=(nt,),
        in_specs=[idx, idx, pl.BlockSpec((tb, pk.shape[1]), lambda i: (i, 0)), row,
                  pl.BlockSpec(memory_space=pl.ANY), vec, vec],
        out_specs=row,
        out_shape=jax.ShapeDtypeStruct((m, d), F32),
        scratch_shapes=[pltpu.VMEM((2, tb, d), F32), pltpu.SemaphoreType.DMA(())],
        compiler_params=_cparams(("arbitrary",), vm),
        name="moe_combine_ln",
    )(d1.reshape(nt, 1, tb), d2.reshape(nt, 1, tb), pk, h, ys, lg.reshape(1, d), lb.reshape(1, d))


def _rope_tables(n_tok):
    pos = jnp.arange(n_tok, dtype=F32)
    inv = ROPE_THETA ** (-jnp.arange(0, MLA_ROPE_DIM, 2, dtype=F32) / MLA_ROPE_DIM)
    ang = pos[:, None] * inv[None, :]
    cos, sin = jnp.cos(ang), jnp.sin(ang)
    zero = jnp.zeros_like(cos)
    return jnp.concatenate([cos, cos, zero, zero], -1), jnp.concatenate([-sin, sin, zero, zero], -1)


def _rel_bucket(rel):
    nb = N_BUCKETS // 2
    max_exact = nb // 2
    n = jnp.abs(rel)
    large = max_exact + (jnp.log(jnp.maximum(n, 1).astype(F32) / max_exact)
                         / math.log(MAX_DISTANCE / max_exact) * (nb - max_exact)).astype(jnp.int32)
    large = jnp.minimum(large, nb - 1)
    return jnp.where(rel > 0, nb, 0) + jnp.where(n < max_exact, n, large)


def _swa_index_tables(n_meta, n_real):
    nblk = n_real // BLOCK
    band = jnp.arange(3 * BLOCK)
    meta_pos = jnp.arange(n_meta)
    pad = BLOCK - n_meta

    def one(start, q_pos, first_block_only=False):
        r_key = start - BLOCK + band
        in_range = (r_key >= 0) & (r_key < n_real)
        if first_block_only:
            in_range = in_range & (r_key < BLOCK)
        k_pos = jnp.concatenate([n_meta + r_key, meta_pos, jnp.zeros((pad,), jnp.int32)])
        always = jnp.concatenate([jnp.zeros((3 * BLOCK,), bool), jnp.ones((n_meta,), bool), jnp.zeros((pad,), bool)])
        live = jnp.concatenate([in_range, jnp.ones((n_meta,), bool), jnp.zeros((pad,), bool)])
        rel = k_pos[None, :] - q_pos[:, None]
        vis = always[None, :] | (live[None, :] & (jnp.abs(rel) <= WINDOW))
        return _rel_bucket(rel), vis

    q_local = jnp.arange(BLOCK)
    tabs = [one(0, n_meta + q_local),
            one(BLOCK * min(1, nblk - 1), n_meta + BLOCK * min(1, nblk - 1) + q_local),
            one(BLOCK * (nblk - 1), n_meta + BLOCK * (nblk - 1) + q_local),
            one(0, jnp.where(q_local < n_meta, q_local, 0), first_block_only=True)]
    bucket = jnp.stack([t[0] for t in tabs]).astype(jnp.int32)
    vis = jnp.stack([t[1] for t in tabs]).astype(jnp.int32)
    return bucket, vis


def _layer_weights(l, w_in, w_uq, w_ukv, w_proj_a, w_proj_b, w_out, d_model, q_rank, kv_rank):
    hd = SWA_HEADS * SWA_HEAD_DIM
    kvd = SWA_KV_HEADS * SWA_HEAD_DIM
    splits = (q_rank, kv_rank, MLA_ROPE_DIM, hd, kvd, kvd, d_model, d_model)
    off = [0]
    for s in splits:
        off.append(off[-1] + s)
    col = lambda i: w_in[l][:, off[i]:off[i + 1]]
    order = (3, 6, 7, 1, 4, 5, 0)
    w_z = jnp.concatenate([col(i) for i in order], axis=1).astype(BF16)
    z_off = {}
    o = 0
    for i in order:
        z_off[i] = o
        o += splits[i]
    half = MLA_ROPE_DIM // 2
    kr = col(2)
    w_kr = jnp.concatenate([kr[:, :half], kr[:, half:], kr[:, :half], kr[:, half:]], axis=1).astype(BF16)
    uq = w_uq[l].reshape(q_rank, MLA_HEADS, MLA_NOPE_DIM + MLA_ROPE_DIM)
    uq_n = uq[:, :, :MLA_NOPE_DIM].reshape(q_rank, MLA_HEADS * MLA_NOPE_DIM).astype(BF16)
    r1 = uq[:, :, MLA_NOPE_DIM:MLA_NOPE_DIM + half]
    r2 = uq[:, :, MLA_NOPE_DIM + half:]
    uq_r = jnp.concatenate([r1, r2, r1, r2], axis=2).reshape(q_rank, MLA_HEADS * V7X_LANES).astype(BF16)
    ukv = w_ukv[l].reshape(kv_rank, MLA_HEADS, MLA_NOPE_DIM + MLA_V_DIM)
    uk = ukv[:, :, :MLA_NOPE_DIM].reshape(kv_rank, MLA_HEADS * MLA_NOPE_DIM).astype(BF16)
    uv = ukv[:, :, MLA_NOPE_DIM:].reshape(kv_rank, MLA_HEADS * MLA_V_DIM).astype(BF16)
    return dict(w_z=w_z, z_off=z_off, w_kr=w_kr, uq_n=uq_n, uq_r=uq_r, uk=uk, uv=uv, uv_t=uv.T,
                wa=w_proj_a[l].astype(BF16), wb=w_proj_b[l].astype(BF16), wo=w_out[l].astype(BF16))


def kernel(x, meta_tokens, emb_ln_g, emb_ln_b, rel_bias, w_in, q_norm_g, kv_norm_g, w_uq, w_ukv, sink_logits,
           w_proj_a, w_proj_b, w_out, ln_mix_g, ln_mix_b, ln_ffn_g, ln_ffn_b, ffn_w1, ffn_w3, ffn_w2, router_w,
           router_b, moe_w1, moe_w3, moe_w2):
    bsz, seq, d = x.shape
    n_meta = meta_tokens.shape[0]
    depth = w_in.shape[0]
    q_rank = q_norm_g.shape[1]
    kv_rank = kv_norm_g.shape[1]
    alpha = (2 * depth) ** 0.25
    nblk = seq // BLOCK
    assert seq % BLOCK == 0 and n_meta % 16 == 0 and n_meta <= BLOCK

    hr, hr_b = _embed_ln(x.reshape(bsz * seq, d), emb_ln_g, emb_ln_b)
    hm1, hm1_b = _embed_ln(meta_tokens.astype(x.dtype), emb_ln_g, emb_ln_b)
    hm, hm_b = jnp.tile(hm1, (bsz, 1)), jnp.tile(hm1_b, (bsz, 1))

    ct, st = _rope_tables(n_meta + seq)
    ct_r, st_r = ct[n_meta:], st[n_meta:]
    ct_m, st_m = jnp.tile(ct[:n_meta], (bsz, 1)), jnp.tile(st[:n_meta], (bsz, 1))
    bucket, vis = _swa_index_tables(n_meta, seq)
    bias_all = _bias_tables(rel_bias, bucket, vis)
    bias_real = bias_all[:3]
    bias_meta = bias_all[3:, :, :n_meta]

    last_blk = nblk - 1
    type_real = lambda i: jnp.where(i == 0, 0, jnp.where(i == last_blk, 2, 1))
    kv_real = lambda i: (jnp.maximum(i - 1, 0), i, jnp.minimum(i + 1, last_blk))
    type_meta = lambda i: 0
    kv_meta_q = lambda i: (0, 0, 0)

    for l in range(depth):
        last = l == depth - 1
        lw = _layer_weights(l, w_in, w_uq, w_ukv, w_proj_a, w_proj_b, w_out, d, q_rank, kv_rank)
        zo = lw["z_off"]
        sink = sink_logits[l]

        z_r = _matmul(hr_b, lw["w_z"], BF16)
        z_m = _matmul(hm_b, lw["w_z"], BF16)
        k_r, v_r = _kv_proj(hr_b, z_r, zo[1], kv_norm_g[l], lw["uk"], lw["uv_t"], lw["w_kr"], ct_r, st_r, True)
        k_m, v_m = _kv_proj(hm_b, z_m, zo[1], kv_norm_g[l], lw["uk"], lw["uv"], lw["w_kr"], ct_m, st_m, False)
        q_r = _q_proj(z_r, zo[0], q_norm_g[l], lw["uq_n"], lw["uq_r"], ct_r, st_r)
        oa_r = _mla_attn(q_r, k_r, k_m, v_r, v_m, bsz, n_meta)
        ob_r = _swa_attn(sink, z_r, BLOCK, z_r, z_m, zo[4], zo[5], bias_real, type_real, kv_real, bsz, nblk, n_meta)
        mg_r = _gate_proj(oa_r, ob_r, z_r, zo[6], zo[7], lw["wa"], lw["wb"])
        hr, hr_b = _out_ln(mg_r, lw["wo"], hr, ln_mix_g[l], ln_mix_b[l], alpha)
        if not last:
            q_m = _q_proj(z_m, zo[0], q_norm_g[l], lw["uq_n"], lw["uq_r"], ct_m, st_m)
            oa_m = _mla_attn(q_m, k_r, k_m, v_r, v_m, bsz, n_meta)
            ob_m = _swa_attn(sink, z_m, n_meta, z_r, z_m, zo[4], zo[5], bias_meta, type_meta, kv_meta_q, bsz, 1,
                             n_meta)
            mg_m = _gate_proj(oa_m, ob_m, z_m, zo[6], zo[7], lw["wa"], lw["wb"])
            hm, hm_b = _out_ln(mg_m, lw["wo"], hm, ln_mix_g[l], ln_mix_b[l], alpha)

        if l % 2 == 0:
            w1, w3, w2 = (ffn_w1[l // 2].astype(BF16), ffn_w3[l // 2].astype(BF16), ffn_w2[l // 2].astype(BF16))
            hr, hr_b = _ffn_down_ln(_ffn_up(hr_b, w1, w3), w2, hr, ln_ffn_g[l], ln_ffn_b[l], alpha)
            if not last:
                hm, hm_b = _ffn_down_ln(_ffn_up(hm_b, w1, w3), w2, hm, ln_ffn_g[l], ln_ffn_b[l], alpha)
        else:
            hr = _moe_layer(hr, router_w[l // 2], router_b[l // 2], moe_w1[l // 2], moe_w3[l // 2], moe_w2[l // 2],
                            ln_ffn_g[l], ln_ffn_b[l], alpha)
            hr_b = hr.astype(BF16)
            if not last:
                hm = _moe_layer(hm, router_w[l // 2], router_b[l // 2], moe_w1[l // 2], moe_w3[l // 2],
                                moe_w2[l // 2], ln_ffn_g[l], ln_ffn_b[l], alpha)
                hm_b = hm.astype(BF16)
    return hr.reshape(bsz, seq, d)


def _moe_layer(h, router_w, router_b, w1, w3, w2, lg, lb, alpha):
    m, d = h.shape
    n_exp = router_w.shape[1]
    tm = _tile(m, 512)
    w_pad = jnp.zeros((d, V7X_LANES), F32).at[:, :n_exp].set(router_w)
    b_pad = jnp.zeros((1, V7X_LANES), F32).at[0, :n_exp].set(router_b)
    pk, cnt = _router(h, w_pad, b_pad, n_exp)
    counts = cnt[0, :n_exp].astype(jnp.int32)
    padded = (counts + tm - 1) // tm * tm
    pad_end = jnp.cumsum(padded)
    pad_start = pad_end - padded
    e1 = pk[:, _PK_E1].astype(jnp.int32)
    e2 = pk[:, _PK_E2].astype(jnp.int32)
    d1 = pad_start[e1] + pk[:, _PK_R1].astype(jnp.int32)
    d2 = pad_start[e2] + pk[:, _PK_R2].astype(jnp.int32)
    n_tiles = (m * TOP_K) // tm + n_exp
    n_slots = n_tiles * tm
    n_used = pad_end[-1] // tm
    tile_id = jnp.arange(n_tiles, dtype=jnp.int32)
    tile_valid = (tile_id < n_used).astype(jnp.int32)
    tile_src = jnp.minimum(tile_id, n_used - 1).astype(jnp.int32)
    tile_expert = jnp.minimum(jnp.searchsorted(pad_end, tile_src * tm, side="right"), n_exp - 1).astype(jnp.int32)
    pad_info = jnp.stack([jnp.append(pad_start + counts, pad_end[-1]),
                          jnp.append(padded - counts, n_slots - pad_end[-1])]).astype(jnp.int32)

    xs = _dispatch(h, d1, d2, pad_info, n_slots)
    ys = _moe_ffn(xs, w1.astype(BF16), w3.astype(BF16), w2.astype(BF16), tile_expert, tile_src, tile_valid, tm)
    return _combine_ln(h, pk, ys, d1, d2, lg, lb, alpha)
```

```python
import functools
import math

import jax
import jax.numpy as jnp
from jax import lax
from jax.experimental import pallas as pl
from jax.experimental.pallas import tpu as pltpu

MLA_HEADS = 16
MLA_NOPE_DIM = 128
MLA_ROPE_DIM = 64
MLA_V_DIM = 128
SWA_HEADS = 16
SWA_KV_HEADS = 4
SWA_HEAD_DIM = 128
WINDOW = 128
BLOCK = 128
N_BUCKETS = 32
MAX_DISTANCE = 128
TOP_K = 2
ROPE_THETA = 10000.0
LN_EPS = 1e-5
RMS_EPS = 1e-6
NEG = -1e30
LOG2E = math.log2(math.e)

V7X_LANES = 128
V7X_MXU_DIM = 256
V7X_VMEM_LIMIT_BYTES = 56 * 1024 * 1024

F32 = jnp.float32
BF16 = jnp.bfloat16


COMPILER_SCRATCH_BYTES = 8 * 1024 * 1024


def _cparams(semantics, vmem_bytes):
    limit = int(min(V7X_VMEM_LIMIT_BYTES, vmem_bytes + COMPILER_SCRATCH_BYTES))
    return pltpu.CompilerParams(dimension_semantics=semantics, vmem_limit_bytes=limit)


def _nbytes(shape, dtype):
    return math.prod(shape) * jnp.dtype(dtype).itemsize


def _dot(a, b):
    return jnp.dot(a, b, preferred_element_type=F32)


def _dot_t(a, b):
    return lax.dot_general(a, b, (((1,), (1,)), ((), ())), preferred_element_type=F32)


def _tile(n, pref):
    t = min(n, pref)
    while n % t:
        t //= 2
    return t


def _ln_rows(y, g, b):
    mu = jnp.mean(y, axis=-1, keepdims=True)
    d = y - mu
    var = jnp.mean(d * d, axis=-1, keepdims=True)
    return d * lax.rsqrt(var + LN_EPS) * g + b


def _embed_ln_kernel(x_ref, g_ref, b_ref, h_ref, hb_ref):
    y = _ln_rows(x_ref[...], g_ref[...], b_ref[...])
    h_ref[...] = y
    hb_ref[...] = y.astype(BF16)


def _embed_ln(x, g, b):
    m, d = x.shape
    tm = _tile(m, 512)
    row = pl.BlockSpec((tm, d), lambda i: (i, 0))
    vec = pl.BlockSpec((1, d), lambda i: (0, 0))
    return pl.pallas_call(
        _embed_ln_kernel,
        grid=(m // tm,),
        in_specs=[row, vec, vec],
        out_specs=[row, row],
        out_shape=[jax.ShapeDtypeStruct((m, d), F32), jax.ShapeDtypeStruct((m, d), BF16)],
        compiler_params=_cparams(("parallel",), 2 * tm * d * 10 + 4 * tm * d * 4),
        name="embed_ln",
    )(x, g.reshape(1, d), b.reshape(1, d))


def _mm_kernel(a_ref, w_ref, o_ref):
    o_ref[...] = _dot(a_ref[...], w_ref[...]).astype(o_ref.dtype)


def _matmul(a, w, out_dtype, tm_pref=1024, tn_pref=768):
    m, k = a.shape
    n = w.shape[1]
    tm = _tile(m, tm_pref)
    tn = next(t for t in (tn_pref, 512, 256, 128) if n % t == 0)
    vm = 2 * (_nbytes((tm, k), BF16) + _nbytes((k, tn), BF16) + _nbytes((tm, tn), out_dtype)) + _nbytes((tm, tn), F32)
    return pl.pallas_call(
        _mm_kernel,
        grid=(m // tm, n // tn),
        in_specs=[pl.BlockSpec((tm, k), lambda i, j: (i, 0)), pl.BlockSpec((k, tn), lambda i, j: (0, j))],
        out_specs=pl.BlockSpec((tm, tn), lambda i, j: (i, j)),
        out_shape=jax.ShapeDtypeStruct((m, n), out_dtype),
        compiler_params=_cparams(("parallel", "parallel"), vm),
        name="in_proj",
    )(a, w)


def _rope(x, ct, st):
    return x * ct + pltpu.roll(x, MLA_ROPE_DIM // 2, 1) * st


def _rms(c, g):
    return c * lax.rsqrt(jnp.mean(c * c, axis=-1, keepdims=True) + RMS_EPS) * g


def _q_proj_kernel(c_ref, g_ref, wn_ref, wr_ref, ct_ref, st_ref, o_ref, *, heads, scale):
    cb = _rms(c_ref[...].astype(F32), g_ref[...]).astype(BF16)
    qn = _dot(cb, wn_ref[...])
    qr = _dot(cb, wr_ref[...])
    ct = ct_ref[...]
    st = st_ref[...]
    w = MLA_NOPE_DIM + V7X_LANES
    for h in range(heads):
        o_ref[:, h * w:h * w + MLA_NOPE_DIM] = (qn[:, h * MLA_NOPE_DIM:(h + 1) * MLA_NOPE_DIM] * scale).astype(BF16)
        xr = qr[:, h * V7X_LANES:(h + 1) * V7X_LANES]
        o_ref[:, h * w + MLA_NOPE_DIM:(h + 1) * w] = (_rope(xr, ct, st) * scale).astype(BF16)


def _q_proj(z, cq_off, g, wn, wr, ct, st):
    m = z.shape[0]
    r = g.shape[0]
    heads = MLA_HEADS
    tm = _tile(math.gcd(m, ct.shape[0]), 512)
    npos = ct.shape[0] // tm
    assert cq_off % r == 0
    wout = heads * (MLA_NOPE_DIM + V7X_LANES)
    scale = (MLA_NOPE_DIM + MLA_ROPE_DIM) ** -0.5 * LOG2E
    vm = (2 * (_nbytes((tm, r), BF16) + _nbytes(wn.shape, BF16) + _nbytes(wr.shape, BF16) + 2 * tm * V7X_LANES * 4
               + _nbytes((tm, wout), BF16)) + 2 * _nbytes((tm, wn.shape[1]), F32) + _nbytes((tm, r), F32) * 2)
    tab = pl.BlockSpec((tm, V7X_LANES), lambda i: (i % npos, 0))
    return pl.pallas_call(
        functools.partial(_q_proj_kernel, heads=heads, scale=scale),
        grid=(m // tm,),
        in_specs=[pl.BlockSpec((tm, r), lambda i: (i, cq_off // r)),
                  pl.BlockSpec((1, r), lambda i: (0, 0)),
                  pl.BlockSpec(wn.shape, lambda i: (0, 0)),
                  pl.BlockSpec(wr.shape, lambda i: (0, 0)),
                  tab, tab],
        out_specs=pl.BlockSpec((tm, wout), lambda i: (i, 0)),
        out_shape=jax.ShapeDtypeStruct((m, wout), BF16),
        compiler_params=_cparams(("parallel",), vm),
        name="mla_q_proj",
    )(z, g.reshape(1, r), wn, wr, ct, st)


def _kv_proj_kernel(hb_ref, c_ref, g_ref, wk_ref, wv_ref, wkr_ref, ct_ref, st_ref, k_ref, v_ref, *, heads,
                    transpose_v):
    cb = _rms(c_ref[...].astype(F32), g_ref[...]).astype(BF16)
    kn = _dot(cb, wk_ref[...])
    if transpose_v:
        v_ref[...] = _dot_t(wv_ref[...], cb).astype(BF16)
    else:
        v_ref[...] = _dot(cb, wv_ref[...]).astype(BF16)
    kr = _rope(_dot(hb_ref[...], wkr_ref[...]), ct_ref[...], st_ref[...]).astype(BF16)
    w = MLA_NOPE_DIM + V7X_LANES
    for h in range(heads):
        k_ref[:, h * w:h * w + MLA_NOPE_DIM] = kn[:, h * MLA_NOPE_DIM:(h + 1) * MLA_NOPE_DIM].astype(BF16)
        k_ref[:, h * w + MLA_NOPE_DIM:(h + 1) * w] = kr


def _kv_proj(hb, z, ckv_off, g, wk, wv, wkr, ct, st, transpose_v):
    m, d = hb.shape
    r = g.shape[0]
    heads = MLA_HEADS
    tm = _tile(math.gcd(m, ct.shape[0]), 512)
    npos = ct.shape[0] // tm
    assert ckv_off % r == 0
    wk_out = heads * (MLA_NOPE_DIM + V7X_LANES)
    wv_out = heads * MLA_V_DIM
    if transpose_v:
        v_spec = pl.BlockSpec((wv_out, tm), lambda i: (0, i))
        v_shape = jax.ShapeDtypeStruct((wv_out, m), BF16)
    else:
        v_spec = pl.BlockSpec((tm, wv_out), lambda i: (i, 0))
        v_shape = jax.ShapeDtypeStruct((m, wv_out), BF16)
    vm = (2 * (_nbytes((tm, d), BF16) + _nbytes((tm, r), BF16) + _nbytes(wk.shape, BF16) + _nbytes(wv.shape, BF16)
               + _nbytes(wkr.shape, BF16) + 2 * tm * V7X_LANES * 4 + _nbytes((tm, wk_out), BF16)
               + _nbytes((tm, wv_out), BF16)) + 2 * _nbytes((tm, wv_out), F32) + _nbytes((tm, r), F32) * 2)
    tab = pl.BlockSpec((tm, V7X_LANES), lambda i: (i % npos, 0))
    return pl.pallas_call(
        functools.partial(_kv_proj_kernel, heads=heads, transpose_v=transpose_v),
        grid=(m // tm,),
        in_specs=[pl.BlockSpec((tm, d), lambda i: (i, 0)),
                  pl.BlockSpec((tm, r), lambda i: (i, ckv_off // r)),
                  pl.BlockSpec((1, r), lambda i: (0, 0)),
                  pl.BlockSpec(wk.shape, lambda i: (0, 0)),
                  pl.BlockSpec(wv.shape, lambda i: (0, 0)),
                  pl.BlockSpec(wkr.shape, lambda i: (0, 0)),
                  tab, tab],
        out_specs=[pl.BlockSpec((tm, wk_out), lambda i: (i, 0)), v_spec],
        out_shape=[jax.ShapeDtypeStruct((m, wk_out), BF16), v_shape],
        compiler_params=_cparams(("parallel",), vm),
        name="mla_kv_proj",
    )(hb, z, g.reshape(1, r), wk, wv, wkr, ct, st)


def _with_ones(v):
    return jnp.concatenate([v, jnp.ones(v.shape, v.dtype)], axis=1)


ONES_ROWS = 16


def _mla_attn_kernel(q_ref, k_ref, km_ref, vt_ref, vm_ref, o_ref, s_ref, m_ref, acc_ref, *, tk, n_meta, hb,
                     unroll):
    wq = MLA_NOPE_DIM + V7X_LANES
    dv = MLA_V_DIM
    tq = s_ref.shape[2]
    n_chunks = k_ref.shape[0] // tk
    pad = V7X_LANES - n_meta
    assert n_chunks % unroll == 0

    def q_of(j):
        q = q_ref[:, j * wq:(j + 1) * wq]
        if q.shape[0] < tq:
            q = jnp.concatenate([q, jnp.zeros((tq - q.shape[0], wq), BF16)], axis=0)
        return q

    def scores(j, c):
        start = pl.multiple_of(c * tk, tk)
        return _dot_t(k_ref[pl.ds(start, tk), j * wq:(j + 1) * wq], q_of(j))

    def accumulate(j, c, s):
        start = pl.multiple_of(c * tk, tk)
        m = m_ref[j, 0:1]
        m_new = jnp.maximum(m, jnp.max(s, axis=0, keepdims=True))
        vt = jnp.concatenate([vt_ref[j * dv:(j + 1) * dv, pl.ds(start, tk)], jnp.ones((ONES_ROWS, tk), BF16)], axis=0)
        acc_ref[j] = jnp.exp2(m - m_new) * acc_ref[j] + _dot(vt, jnp.exp2(s - m_new).astype(BF16))
        m_ref[j, 0:1] = m_new

    for j in range(hb):
        kmp = jnp.concatenate([km_ref[:, j * wq:(j + 1) * wq], jnp.zeros((pad, wq), BF16)], axis=0)
        vmp = jnp.concatenate([vm_ref[:, j * dv:(j + 1) * dv].astype(F32), jnp.zeros((pad, dv), F32)], axis=0)
        vt = jnp.concatenate([vmp.T.astype(BF16), jnp.ones((ONES_ROWS, V7X_LANES), BF16)], axis=0)
        s = _dot_t(kmp, q_of(j))
        row = lax.broadcasted_iota(jnp.int32, s.shape, 0)
        s = jnp.where(row < n_meta, s, NEG)
        m0 = jnp.max(s, axis=0, keepdims=True)
        m_ref[j, 0:1] = m0
        acc_ref[j] = _dot(vt, jnp.exp2(s - m0).astype(BF16))
        s_ref[j] = scores(j, 0)

    def group(c0, prefetch):
        cur = [s_ref[j] for j in range(hb)]
        for u in range(unroll):
            nxt = None
            if u < unroll - 1:
                nxt = [scores(j, c0 + u + 1) for j in range(hb)]
            elif prefetch:
                for j in range(hb):
                    s_ref[j] = scores(j, c0 + unroll)
            for j in range(hb):
                accumulate(j, c0 + u, cur[j])
            cur = nxt

    def body(i, carry):
        group(unroll * i, True)
        return carry

    lax.fori_loop(0, n_chunks // unroll - 1, body, 0)
    group(n_chunks - unroll, False)
    for j in range(hb):
        acc = acc_ref[j]
        o = (acc[:dv] / acc[dv:dv + 1]).T
        o_ref[:, j * dv:(j + 1) * dv] = o[:o_ref.shape[0]].astype(o_ref.dtype)


def _mla_attn(q, k, km, vt, vmeta, batch, n_meta):
    heads = MLA_HEADS
    hb = 2
    wq = MLA_NOPE_DIM + V7X_LANES
    dv = MLA_V_DIM
    rows_q = q.shape[0] // batch
    seq = k.shape[0] // batch
    tq = _tile(rows_q, 512)
    tk = _tile(seq, V7X_MXU_DIM)
    unroll = _tile(seq // tk, 16)
    nq = rows_q // tq
    tqp = max(tq, V7X_LANES)
    vm = (2 * hb * (_nbytes((tq, wq), BF16) + _nbytes((seq, wq), BF16) + _nbytes((seq, dv), BF16)
                    + _nbytes((tq, dv), BF16)) + hb * (6 * _nbytes((tk, tqp), F32) + 2 * _nbytes((2 * dv, tqp), F32)))
    return pl.pallas_call(
        functools.partial(_mla_attn_kernel, tk=tk, n_meta=n_meta, hb=hb, unroll=unroll),
        grid=(batch, heads // hb, nq),
        in_specs=[pl.BlockSpec((tq, hb * wq), lambda b, h, i: (b * nq + i, h)),
                  pl.BlockSpec((seq, hb * wq), lambda b, h, i: (b, h)),
                  pl.BlockSpec((n_meta, hb * wq), lambda b, h, i: (b, h)),
                  pl.BlockSpec((hb * dv, seq), lambda b, h, i: (h, b)),
                  pl.BlockSpec((n_meta, hb * dv), lambda b, h, i: (b, h))],
        out_specs=pl.BlockSpec((tq, hb * dv), lambda b, h, i: (b * nq + i, h)),
        out_shape=jax.ShapeDtypeStruct((q.shape[0], heads * dv), BF16),
        scratch_shapes=[pltpu.VMEM((hb, tk, tqp), F32), pltpu.VMEM((hb, 8, tqp), F32),
                        pltpu.VMEM((hb, dv + ONES_ROWS, tqp), F32)],
        compiler_params=_cparams(("parallel", "parallel", "parallel"), vm),
        name="mla_attn",
    )(q, k, km, vt, vmeta)


def _bias_kernel(rb_ref, bk_ref, vis_ref, o_ref, *, n_buckets):
    h = pl.program_id(1)
    bk = bk_ref[0]
    acc = jnp.zeros(bk.shape, F32)
    for j in range(n_buckets):
        acc = jnp.where(bk == j, rb_ref[j, h] * LOG2E, acc)
    o_ref[0, 0] = jnp.where(vis_ref[0] != 0, acc, NEG)


def _bias_tables(rel_bias, bucket, vis):
    t, r, s = bucket.shape
    heads = rel_bias.shape[1]
    blk = pl.BlockSpec((1, r, s), lambda i, h: (i, 0, 0))
    return pl.pallas_call(
        functools.partial(_bias_kernel, n_buckets=rel_bias.shape[0]),
        grid=(t, heads),
        in_specs=[pl.BlockSpec(memory_space=pltpu.SMEM), blk, blk],
        out_specs=pl.BlockSpec((1, 1, r, s), lambda i, h: (i, h, 0, 0)),
        out_shape=jax.ShapeDtypeStruct((t, heads, r, s), F32),
        compiler_params=_cparams(("parallel", "parallel"), 16 * r * s * 4),
        name="swa_bias_table",
    )(rel_bias, bucket, vis)


def _swa_kernel(sink_ref, q_ref, kp_ref, ko_ref, kn_ref, km_ref, vp_ref, vo_ref, vn_ref, vm_ref, bias_ref, o_ref,
                *, kv_heads, group, scale, n_meta):
    d = SWA_HEAD_DIM
    r = q_ref.shape[0]
    zpad = jnp.zeros((BLOCK - n_meta, d), BF16)

    for g in range(kv_heads):
        cols = slice(g * d, (g + 1) * d)
        kcat = jnp.concatenate([kp_ref[:, cols], ko_ref[:, cols], kn_ref[:, cols], km_ref[:, cols], zpad], axis=0)
        vcat = jnp.concatenate([vp_ref[:, cols], vo_ref[:, cols], vn_ref[:, cols], vm_ref[:, cols], zpad], axis=0)
        heads = range(g * group, (g + 1) * group)
        q = jnp.concatenate([q_ref[:, h * d:(h + 1) * d] for h in heads], axis=0)
        bias = jnp.concatenate([bias_ref[0, h] for h in heads], axis=0)
        sink = jnp.concatenate([jnp.full((r, 1), sink_ref[h] * LOG2E, F32) for h in heads], axis=0)
        s = _dot_t(q, kcat) * scale + bias
        m = jnp.maximum(jnp.max(s, axis=-1, keepdims=True), sink)
        acc = _dot(jnp.exp2(s - m).astype(BF16), _with_ones(vcat))
        o = acc[:, :d] / (acc[:, d:] + jnp.exp2(sink - m))
        for j, h in enumerate(heads):
            o_ref[:, h * d:(h + 1) * d] = o[j * r:(j + 1) * r].astype(o_ref.dtype)


def _swa_attn(sink, q_src, q_rows, kv_src, kv_meta, k_off, v_off, bias, type_of_block, kv_block_of, batch, n_grid_blk,
              n_meta):
    d = SWA_HEAD_DIM
    group = SWA_HEADS // SWA_KV_HEADS
    wq, wkv = SWA_HEADS * d, SWA_KV_HEADS * d
    nblk_kv = kv_src.shape[0] // batch // BLOCK
    s_keys = bias.shape[-1]
    assert k_off % wkv == 0 and v_off % wkv == 0
    kc, vc = k_off // wkv, v_off // wkv

    def kv_spec(which, col):
        return pl.BlockSpec((BLOCK, wkv), lambda b, i: (b * nblk_kv + kv_block_of(i)[which], col))

    def meta_spec(col):
        return pl.BlockSpec((n_meta, wkv), lambda b, i: (b, col))

    rows = group * q_rows
    vm = (2 * (2 * _nbytes((q_rows, wq), BF16) + 8 * _nbytes((BLOCK, wkv), BF16)
               + _nbytes((SWA_HEADS, q_rows, s_keys), F32))
          + SWA_KV_HEADS * (4 * _nbytes((rows, s_keys), F32) + _nbytes((rows, 2 * d), F32)))
    return pl.pallas_call(
        functools.partial(_swa_kernel, kv_heads=SWA_KV_HEADS, group=group, scale=d ** -0.5 * LOG2E, n_meta=n_meta),
        grid=(batch, n_grid_blk),
        in_specs=[pl.BlockSpec(memory_space=pltpu.SMEM),
                  pl.BlockSpec((q_rows, wq), lambda b, i: (b * n_grid_blk + i, 0)),
                  kv_spec(0, kc), kv_spec(1, kc), kv_spec(2, kc), meta_spec(kc),
                  kv_spec(0, vc), kv_spec(1, vc), kv_spec(2, vc), meta_spec(vc),
                  pl.BlockSpec((1, SWA_HEADS, q_rows, s_keys), lambda b, i: (type_of_block(i), 0, 0, 0))],
        out_specs=pl.BlockSpec((q_rows, wq), lambda b, i: (b * n_grid_blk + i, 0)),
        out_shape=jax.ShapeDtypeStruct((batch * n_grid_blk * q_rows, wq), BF16),
        compiler_params=_cparams(("parallel", "parallel"), vm),
        name="swa_attn",
    )(sink, q_src, kv_src, kv_src, kv_src, kv_meta, kv_src, kv_src, kv_src, kv_meta, bias)


def _gate_proj_kernel(oa_ref, ob_ref, ga_ref, gb_ref, wa_ref, wb_ref, o_ref):
    pa = _dot(oa_ref[...], wa_ref[...])
    pb = _dot(ob_ref[...], wb_ref[...])
    ga = jax.nn.sigmoid(ga_ref[...].astype(F32))
    gb = jax.nn.sigmoid(gb_ref[...].astype(F32))
    o_ref[...] = (ga * pa + gb * pb).astype(o_ref.dtype)


def _gate_proj(oa, ob, z, ga_off, gb_off, wa, wb):
    m, ka = oa.shape
    kb = ob.shape[1]
    n = wa.shape[1]
    tm = _tile(m, 1024)
    tn = _tile(n, 512)
    assert ga_off % tn == 0 and gb_off % tn == 0
    vm = (2 * (_nbytes((tm, ka + kb), BF16) + _nbytes((ka + kb, tn), BF16) + 3 * _nbytes((tm, tn), BF16))
          + 4 * _nbytes((tm, tn), F32))
    act = lambda k: pl.BlockSpec((tm, k), lambda i, j: (i, 0))
    wsp = lambda k: pl.BlockSpec((k, tn), lambda i, j: (0, j))
    return pl.pallas_call(
        _gate_proj_kernel,
        grid=(m // tm, n // tn),
        in_specs=[act(ka), act(kb),
                  pl.BlockSpec((tm, tn), lambda i, j: (i, ga_off // tn + j)),
                  pl.BlockSpec((tm, tn), lambda i, j: (i, gb_off // tn + j)),
                  wsp(ka), wsp(kb)],
        out_specs=pl.BlockSpec((tm, tn), lambda i, j: (i, j)),
        out_shape=jax.ShapeDtypeStruct((m, n), BF16),
        compiler_params=_cparams(("parallel", "parallel"), vm),
        name="gate_proj",
    )(oa, ob, z, z, wa, wb)


def _out_ln_kernel(m_ref, w_ref, h_ref, g_ref, b_ref, ho_ref, *, alpha):
    y = alpha * h_ref[...] + _dot(m_ref[...], w_ref[...])
    ho_ref[...] = _ln_rows(y, g_ref[...], b_ref[...])


def _out_ln(merged, w, h, g, b, alpha):
    m, d = h.shape
    tm = _tile(m, 512)
    row = pl.BlockSpec((tm, d), lambda i: (i, 0))
    vec = pl.BlockSpec((1, d), lambda i: (0, 0))
    vm = 2 * (_nbytes((tm, d), BF16) + _nbytes(w.shape, BF16) + 2 * _nbytes((tm, d), F32)) + 3 * _nbytes((tm, d), F32)
    return pl.pallas_call(
        functools.partial(_out_ln_kernel, alpha=alpha),
        grid=(m // tm,),
        in_specs=[row, pl.BlockSpec(w.shape, lambda i: (0, 0)), row, vec, vec],
        out_specs=row,
        out_shape=jax.ShapeDtypeStruct((m, d), F32),
        compiler_params=_cparams(("parallel",), vm),
        name="out_proj_ln",
    )(merged, w, h, g.reshape(1, d), b.reshape(1, d))


def _swiglu(a1, a3):
    return a1 * jax.nn.sigmoid(a1) * a3


def _ffn_ln_kernel(h_ref, w1_ref, w3_ref, w2_ref, lg_ref, lb_ref, ho_ref, hbo_ref, xb_ref, acc_ref, *, alpha):
    j = pl.program_id(1)

    @pl.when(j == 0)
    def _():
        xb_ref[...] = h_ref[...].astype(BF16)
        acc_ref[...] = jnp.zeros_like(acc_ref)

    xb = xb_ref[...]
    gact = _swiglu(_dot(xb, w1_ref[...]), _dot(xb, w3_ref[...])).astype(BF16)
    acc_ref[...] += _dot(gact, w2_ref[...])

    @pl.when(j == pl.num_programs(1) - 1)
    def _():
        y = _ln_rows(alpha * h_ref[...] + acc_ref[...], lg_ref[...], lb_ref[...])
        ho_ref[...] = y
        hbo_ref[...] = y.astype(BF16)


def _ffn_ln(h, w1, w3, w2, lg, lb, alpha):
    m, d = h.shape
    f = w1.shape[1]
    tm = _tile(m, 512)
    tf = _tile(f, 512)
    vm = (2 * (2 * _nbytes((tm, d), F32) + 3 * _nbytes((d, tf), BF16) + _nbytes((tm, d), BF16))
          + _nbytes((tm, d), BF16) + _nbytes((tm, d), F32) + 4 * _nbytes((tm, tf), F32) + 2 * _nbytes((tm, d), F32))
    row = pl.BlockSpec((tm, d), lambda i, j: (i, 0))
    vec = pl.BlockSpec((1, d), lambda i, j: (0, 0))
    wup = pl.BlockSpec((d, tf), lambda i, j: (0, j))
    return pl.pallas_call(
        functools.partial(_ffn_ln_kernel, alpha=alpha),
        grid=(m // tm, f // tf),
        in_specs=[row, wup, wup, pl.BlockSpec((tf, d), lambda i, j: (j, 0)), vec, vec],
        out_specs=[row, row],
        out_shape=[jax.ShapeDtypeStruct((m, d), F32), jax.ShapeDtypeStruct((m, d), BF16)],
        scratch_shapes=[pltpu.VMEM((tm, d), BF16), pltpu.VMEM((tm, d), F32)],
        compiler_params=_cparams(("parallel", "arbitrary"), vm),
        name="ffn_ln",
    )(h, w1, w3, w2, lg.reshape(1, d), lb.reshape(1, d))


_PK_E1, _PK_E2, _PK_G1, _PK_G2, _PK_R1, _PK_R2 = range(6)


def _router_kernel(h_ref, w_ref, b_ref, pk_ref, cnt_ref, carry_ref, *, n_exp):
    @pl.when(pl.program_id(0) == 0)
    def _():
        carry_ref[...] = jnp.zeros_like(carry_ref)

    logits = jnp.dot(h_ref[...], w_ref[...], preferred_element_type=F32, precision=lax.Precision.HIGHEST) + b_ref[...]
    tm, nl = logits.shape
    lane = lax.broadcasted_iota(jnp.int32, (tm, nl), 1).astype(F32)
    s = jnp.where(lane < n_exp, logits, -jnp.inf)
    m1 = jnp.max(s, axis=-1, keepdims=True)
    i1 = jnp.min(jnp.where(s == m1, lane, float(nl)), axis=-1, keepdims=True)
    s2 = jnp.where(lane == i1, -jnp.inf, s)
    m2 = jnp.max(s2, axis=-1, keepdims=True)
    i2 = jnp.min(jnp.where(s2 == m2, lane, float(nl)), axis=-1, keepdims=True)
    e = jnp.exp(m2 - m1)
    g1 = 1.0 / (1.0 + e)
    g2 = e / (1.0 + e)
    oh1 = lane == i1
    oh2 = lane == i2
    both = jnp.where(oh1 | oh2, 1.0, 0.0)
    r = lax.broadcasted_iota(jnp.int32, (tm, tm), 0)
    c = lax.broadcasted_iota(jnp.int32, (tm, tm), 1)
    tri = jnp.where(c < r, 1.0, 0.0).astype(BF16)
    prefix = _dot(tri, both.astype(BF16)) + carry_ref[...]
    r1 = jnp.sum(jnp.where(oh1, prefix, 0.0), axis=-1, keepdims=True)
    r2 = jnp.sum(jnp.where(oh2, prefix, 0.0), axis=-1, keepdims=True)
    carry_ref[...] += jnp.sum(both, axis=0, keepdims=True)
    cnt_ref[...] = carry_ref[...]
    pk = jnp.zeros((tm, nl), F32)
    for col, val in ((_PK_E1, i1), (_PK_E2, i2), (_PK_G1, g1), (_PK_G2, g2), (_PK_R1, r1), (_PK_R2, r2)):
        pk = jnp.where(lane == col, val, pk)
    pk_ref[...] = pk


def _router(h, w_pad, b_pad, n_exp):
    m, d = h.shape
    nl = w_pad.shape[1]
    tm = _tile(m, 512)
    vm = 2 * (_nbytes((tm, d), F32) + _nbytes((d, nl), F32) + _nbytes((tm, nl), F32)) + _nbytes((tm, tm), F32) * 3
    return pl.pallas_call(
        functools.partial(_router_kernel, n_exp=n_exp),
        grid=(m // tm,),
        in_specs=[pl.BlockSpec((tm, d), lambda i: (i, 0)), pl.BlockSpec((d, nl), lambda i: (0, 0)),
                  pl.BlockSpec((1, nl), lambda i: (0, 0))],
        out_specs=[pl.BlockSpec((tm, nl), lambda i: (i, 0)), pl.BlockSpec((1, nl), lambda i: (0, 0))],
        out_shape=[jax.ShapeDtypeStruct((m, nl), F32), jax.ShapeDtypeStruct((1, nl), F32)],
        scratch_shapes=[pltpu.VMEM((1, nl), F32)],
        compiler_params=_cparams(("arbitrary",), vm),
        name="moe_router",
    )(h, w_pad, b_pad)


def _row_copy(src_ref, src_row, dst_ref, dst_row, sem):
    return pltpu.make_async_copy(src_ref.at[pl.ds(src_row, 1)], dst_ref.at[pl.ds(dst_row, 1)], sem)


DMA_ISSUE_UNROLL = 8


def _dispatch_kernel(d1_ref, d2_ref, pad_ref, h_ref, zero_ref, xs_ref, sem, *, n_ranges, tb):
    i = pl.program_id(0)
    row0 = i * tb

    def issue(t, carry):
        _row_copy(h_ref, row0 + t, xs_ref, d1_ref[0, 0, t], sem).start()
        _row_copy(h_ref, row0 + t, xs_ref, d2_ref[0, 0, t], sem).start()
        return carry

    def drain_step():
        pltpu.make_async_copy(h_ref.at[pl.ds(0, 2 * tb)], xs_ref.at[pl.ds(0, 2 * tb)], sem).wait()

    lax.fori_loop(0, tb, issue, 0, unroll=DMA_ISSUE_UNROLL)

    @pl.when(i == 0)
    def _():
        for e in range(n_ranges):
            base = pad_ref[0, e]
            n = pad_ref[1, e]
            lax.fori_loop(0, n, lambda j, c: (_row_copy(zero_ref, 0, xs_ref, base + j, sem).start(), c)[1], 0)
            lax.fori_loop(0, n, lambda j, c: (_row_copy(zero_ref, 0, xs_ref, 0, sem).wait(), c)[1], 0)

    @pl.when(i > 0)
    def _():
        drain_step()

    @pl.when(i == pl.num_programs(0) - 1)
    def _():
        drain_step()


def _dispatch(h, d1, d2, pad_info, n_slots):
    m, d = h.shape
    tb = _tile(m, 256)
    nt = m // tb
    idx = pl.BlockSpec((1, 1, tb), lambda i: (i, 0, 0), memory_space=pltpu.SMEM)
    zero = jnp.zeros((8, d), F32)
    return pl.pallas_call(
        functools.partial(_dispatch_kernel, n_ranges=pad_info.shape[1], tb=tb),
        grid=(nt,),
        in_specs=[idx, idx, pl.BlockSpec(memory_space=pltpu.SMEM), pl.BlockSpec(memory_space=pl.ANY),
                  pl.BlockSpec((8, d), lambda i: (0, 0))],
        out_specs=pl.BlockSpec(memory_space=pl.ANY),
        out_shape=jax.ShapeDtypeStruct((n_slots, d), F32),
        scratch_shapes=[pltpu.SemaphoreType.DMA(())],
        compiler_params=_cparams(("arbitrary",), 4 * tb * d * 4),
        name="moe_dispatch",
    )(d1.reshape(nt, 1, tb), d2.reshape(nt, 1, tb), pad_info, h, zero)


def _moe_ffn_kernel(te_ref, ts_ref, tv_ref, x_ref, w1_ref, w3_ref, w2_ref, y_ref, xb_ref):
    i = pl.program_id(0)
    j = pl.program_id(1)

    @pl.when(tv_ref[i] != 0)
    def _():
        @pl.when(j == 0)
        def _():
            xb_ref[...] = x_ref[...].astype(BF16)
            y_ref[...] = jnp.zeros_like(y_ref)

        xb = xb_ref[...]
        gact = _swiglu(_dot(xb, w1_ref[0]), _dot(xb, w3_ref[0])).astype(BF16)
        y_ref[...] += _dot(gact, w2_ref[0])

    @pl.when((tv_ref[i] == 0) & (j == 0))
    def _():
        y_ref[...] = jnp.zeros_like(y_ref)


def _moe_ffn(xs, w1, w3, w2, tile_expert, tile_src, tile_valid, tm):
    n_slots, d = xs.shape
    f = w1.shape[2]
    tf = _tile(f, 512)
    nf = f // tf
    n_tiles = n_slots // tm

    def jj(i, j, tv):
        return jnp.where(tv[i] != 0, j, nf - 1)

    vm = (2 * (2 * _nbytes((tm, d), F32) + 3 * _nbytes((d, tf), BF16)) + _nbytes((tm, d), BF16)
          + 4 * _nbytes((tm, tf), F32) + _nbytes((tm, d), F32))
    grid_spec = pltpu.PrefetchScalarGridSpec(
        num_scalar_prefetch=3,
        grid=(n_tiles, nf),
        in_specs=[pl.BlockSpec((tm, d), lambda i, j, te, ts, tv: (ts[i], 0)),
                  pl.BlockSpec((1, d, tf), lambda i, j, te, ts, tv: (te[i], 0, jj(i, j, tv))),
                  pl.BlockSpec((1, d, tf), lambda i, j, te, ts, tv: (te[i], 0, jj(i, j, tv))),
                  pl.BlockSpec((1, tf, d), lambda i, j, te, ts, tv: (te[i], jj(i, j, tv), 0))],
        out_specs=pl.BlockSpec((tm, d), lambda i, j, te, ts, tv: (i, 0)),
        scratch_shapes=[pltpu.VMEM((tm, d), BF16)],
    )
    return pl.pallas_call(
        _moe_ffn_kernel,
        grid_spec=grid_spec,
        out_shape=jax.ShapeDtypeStruct((n_slots, d), F32),
        compiler_params=_cparams(("arbitrary", "arbitrary"), vm),
        name="moe_expert_ffn",
    )(tile_expert, tile_src, tile_valid, xs, w1, w3, w2)


def _combine_ln_kernel(d1_ref, d2_ref, d1n_ref, d2n_ref, pk_ref, h_ref, y_ref, lg_ref, lb_ref, o_ref, ybuf_ref, sems,
                       *, alpha, tb):
    step = pl.program_id(0)

    def gather(i1_ref, i2_ref, off, buf):
        def issue(t, carry):
            _row_copy(y_ref, i1_ref[0, 0, off + t], ybuf_ref.at[buf, 0], t, sems.at[buf]).start()
            _row_copy(y_ref, i2_ref[0, 0, off + t], ybuf_ref.at[buf, 1], t, sems.at[buf]).start()
            return carry

        lax.fori_loop(0, tb, issue, 0, unroll=DMA_ISSUE_UNROLL)

    def wait(buf):
        for k in range(TOP_K):
            pltpu.make_async_copy(y_ref.at[pl.ds(0, tb)], ybuf_ref.at[buf, k], sems.at[buf]).wait()

    def finish(buf, rows):
        pk = pk_ref[rows, :]
        f = pk[:, _PK_G1:_PK_G1 + 1] * ybuf_ref[buf, 0] + pk[:, _PK_G2:_PK_G2 + 1] * ybuf_ref[buf, 1]
        o_ref[rows, :] = _ln_rows(alpha * h_ref[rows, :] + f, lg_ref[...], lb_ref[...])

    @pl.when(step == 0)
    def _():
        gather(d1_ref, d2_ref, 0, 0)

    gather(d1_ref, d2_ref, tb, 1)
    wait(0)
    finish(0, slice(0, tb))

    @pl.when(step < pl.num_programs(0) - 1)
    def _():
        gather(d1n_ref, d2n_ref, 0, 0)

    wait(1)
    finish(1, slice(tb, 2 * tb))


def _combine_ln(h, pk, ys, d1, d2, lg, lb, alpha):
    m, d = h.shape
    tb = _tile(m // 2, 256)
    ns = m // (2 * tb)
    idx = pl.BlockSpec((1, 1, 2 * tb), lambda i: (i, 0, 0), memory_space=pltpu.SMEM)
    idx_next = pl.BlockSpec((1, 1, 2 * tb), lambda i: (jnp.minimum(i + 1, ns - 1), 0, 0), memory_space=pltpu.SMEM)
    row = pl.BlockSpec((2 * tb, d), lambda i: (i, 0))
    vec = pl.BlockSpec((1, d), lambda i: (0, 0))
    vm = (2 * (2 * _nbytes((2 * tb, d), F32) + _nbytes((2 * tb, pk.shape[1]), F32)) + 4 * _nbytes((tb, d), F32)
          + 4 * _nbytes((tb, d), F32))
    d1b, d2b = d1.reshape(ns, 1, 2 * tb), d2.reshape(ns, 1, 2 * tb)
    return pl.pallas_call(
        functools.partial(_combine_ln_kernel, alpha=alpha, tb=tb),
        grid=(ns,),
        in_specs=[idx, idx, idx_next, idx_next, pl.BlockSpec((2 * tb, pk.shape[1]), lambda i: (i, 0)), row,
                  pl.BlockSpec(memory_space=pl.ANY), vec, vec],
        out_specs=row,
        out_shape=jax.ShapeDtypeStruct((m, d), F32),
        scratch_shapes=[pltpu.VMEM((2, TOP_K, tb, d), F32), pltpu.SemaphoreType.DMA((2,))],
        compiler_params=_cparams(("arbitrary",), vm),
        name="moe_combine_ln",
    )(d1b, d2b, d1b, d2b, pk, h, ys, lg.reshape(1, d), lb.reshape(1, d))


def _rope_tables(n_tok):
    pos = jnp.arange(n_tok, dtype=F32)
    inv = ROPE_THETA ** (-jnp.arange(0, MLA_ROPE_DIM, 2, dtype=F32) / MLA_ROPE_DIM)
    ang = pos[:, None] * inv[None, :]
    cos, sin = jnp.cos(ang), jnp.sin(ang)
    zero = jnp.zeros_like(cos)
    return jnp.concatenate([cos, cos, zero, zero], -1), jnp.concatenate([-sin, sin, zero, zero], -1)


def _rel_bucket(rel):
    nb = N_BUCKETS // 2
    max_exact = nb // 2
    n = jnp.abs(rel)
    large = max_exact + (jnp.log(jnp.maximum(n, 1).astype(F32) / max_exact)
                         / math.log(MAX_DISTANCE / max_exact) * (nb - max_exact)).astype(jnp.int32)
    large = jnp.minimum(large, nb - 1)
    return jnp.where(rel > 0, nb, 0) + jnp.where(n < max_exact, n, large)


def _swa_index_tables(n_meta, n_real):
    nblk = n_real // BLOCK
    band = jnp.arange(3 * BLOCK)
    meta_pos = jnp.arange(n_meta)
    pad = BLOCK - n_meta

    def one(start, q_pos, first_block_only=False):
        r_key = start - BLOCK + band
        in_range = (r_key >= 0) & (r_key < n_real)
        if first_block_only:
            in_range = in_range & (r_key < BLOCK)
        k_pos = jnp.concatenate([n_meta + r_key, meta_pos, jnp.zeros((pad,), jnp.int32)])
        always = jnp.concatenate([jnp.zeros((3 * BLOCK,), bool), jnp.ones((n_meta,), bool), jnp.zeros((pad,), bool)])
        live = jnp.concatenate([in_range, jnp.ones((n_meta,), bool), jnp.zeros((pad,), bool)])
        rel = k_pos[None, :] - q_pos[:, None]
        vis = always[None, :] | (live[None, :] & (jnp.abs(rel) <= WINDOW))
        return _rel_bucket(rel), vis

    q_local = jnp.arange(BLOCK)
    tabs = [one(0, n_meta + q_local),
            one(BLOCK * min(1, nblk - 1), n_meta + BLOCK * min(1, nblk - 1) + q_local),
            one(BLOCK * (nblk - 1), n_meta + BLOCK * (nblk - 1) + q_local),
            one(0, jnp.where(q_local < n_meta, q_local, 0), first_block_only=True)]
    bucket = jnp.stack([t[0] for t in tabs]).astype(jnp.int32)
    vis = jnp.stack([t[1] for t in tabs]).astype(jnp.int32)
    return bucket, vis


def _layer_weights(l, w_in, w_uq, w_ukv, w_proj_a, w_proj_b, w_out, d_model, q_rank, kv_rank):
    hd = SWA_HEADS * SWA_HEAD_DIM
    kvd = SWA_KV_HEADS * SWA_HEAD_DIM
    splits = (q_rank, kv_rank, MLA_ROPE_DIM, hd, kvd, kvd, d_model, d_model)
    off = [0]
    for s in splits:
        off.append(off[-1] + s)
    col = lambda i: w_in[l][:, off[i]:off[i + 1]]
    order = (3, 6, 7, 1, 4, 5, 0)
    w_z = jnp.concatenate([col(i) for i in order], axis=1).astype(BF16)
    z_off = {}
    o = 0
    for i in order:
        z_off[i] = o
        o += splits[i]
    half = MLA_ROPE_DIM // 2
    kr = col(2)
    w_kr = jnp.concatenate([kr[:, :half], kr[:, half:], kr[:, :half], kr[:, half:]], axis=1).astype(BF16)
    uq = w_uq[l].reshape(q_rank, MLA_HEADS, MLA_NOPE_DIM + MLA_ROPE_DIM)
    uq_n = uq[:, :, :MLA_NOPE_DIM].reshape(q_rank, MLA_HEADS * MLA_NOPE_DIM).astype(BF16)
    r1 = uq[:, :, MLA_NOPE_DIM:MLA_NOPE_DIM + half]
    r2 = uq[:, :, MLA_NOPE_DIM + half:]
    uq_r = jnp.concatenate([r1, r2, r1, r2], axis=2).reshape(q_rank, MLA_HEADS * V7X_LANES).astype(BF16)
    ukv = w_ukv[l].reshape(kv_rank, MLA_HEADS, MLA_NOPE_DIM + MLA_V_DIM)
    uk = ukv[:, :, :MLA_NOPE_DIM].reshape(kv_rank, MLA_HEADS * MLA_NOPE_DIM).astype(BF16)
    uv = ukv[:, :, MLA_NOPE_DIM:].reshape(kv_rank, MLA_HEADS * MLA_V_DIM).astype(BF16)
    return dict(w_z=w_z, z_off=z_off, w_kr=w_kr, uq_n=uq_n, uq_r=uq_r, uk=uk, uv=uv, uv_t=uv.T,
                wa=w_proj_a[l].astype(BF16), wb=w_proj_b[l].astype(BF16), wo=w_out[l].astype(BF16))


def kernel(x, meta_tokens, emb_ln_g, emb_ln_b, rel_bias, w_in, q_norm_g, kv_norm_g, w_uq, w_ukv, sink_logits,
           w_proj_a, w_proj_b, w_out, ln_mix_g, ln_mix_b, ln_ffn_g, ln_ffn_b, ffn_w1, ffn_w3, ffn_w2, router_w,
           router_b, moe_w1, moe_w3, moe_w2):
    bsz, seq, d = x.shape
    n_meta = meta_tokens.shape[0]
    depth = w_in.shape[0]
    q_rank = q_norm_g.shape[1]
    kv_rank = kv_norm_g.shape[1]
    alpha = (2 * depth) ** 0.25
    nblk = seq // BLOCK
    assert seq % BLOCK == 0 and n_meta % 16 == 0 and n_meta <= BLOCK

    hr, hr_b = _embed_ln(x.reshape(bsz * seq, d), emb_ln_g, emb_ln_b)
    hm1, hm1_b = _embed_ln(meta_tokens.astype(x.dtype), emb_ln_g, emb_ln_b)
    hm, hm_b = jnp.tile(hm1, (bsz, 1)), jnp.tile(hm1_b, (bsz, 1))

    ct, st = _rope_tables(n_meta + seq)
    ct_r, st_r = ct[n_meta:], st[n_meta:]
    ct_m, st_m = jnp.tile(ct[:n_meta], (bsz, 1)), jnp.tile(st[:n_meta], (bsz, 1))
    bucket, vis = _swa_index_tables(n_meta, seq)
    bias_all = _bias_tables(rel_bias, bucket, vis)
    bias_real = bias_all[:3]
    bias_meta = bias_all[3:, :, :n_meta]

    last_blk = nblk - 1
    type_real = lambda i: jnp.where(i == 0, 0, jnp.where(i == last_blk, 2, 1))
    kv_real = lambda i: (jnp.maximum(i - 1, 0), i, jnp.minimum(i + 1, last_blk))
    type_meta = lambda i: 0
    kv_meta_q = lambda i: (0, 0, 0)

    for l in range(depth):
        last = l == depth - 1
        lw = _layer_weights(l, w_in, w_uq, w_ukv, w_proj_a, w_proj_b, w_out, d, q_rank, kv_rank)
        zo = lw["z_off"]
        sink = sink_logits[l]

        z_r = _matmul(hr_b, lw["w_z"], BF16)
        z_m = _matmul(hm_b, lw["w_z"], BF16)
        k_r, v_r = _kv_proj(hr_b, z_r, zo[1], kv_norm_g[l], lw["uk"], lw["uv_t"], lw["w_kr"], ct_r, st_r, True)
        k_m, v_m = _kv_proj(hm_b, z_m, zo[1], kv_norm_g[l], lw["uk"], lw["uv"], lw["w_kr"], ct_m, st_m, False)
        q_r = _q_proj(z_r, zo[0], q_norm_g[l], lw["uq_n"], lw["uq_r"], ct_r, st_r)
        oa_r = _mla_attn(q_r, k_r, k_m, v_r, v_m, bsz, n_meta)
        ob_r = _swa_attn(sink, z_r, BLOCK, z_r, z_m, zo[4], zo[5], bias_real, type_real, kv_real, bsz, nblk, n_meta)
        mg_r = _gate_proj(oa_r, ob_r, z_r, zo[6], zo[7], lw["wa"], lw["wb"])
        hr = _out_ln(mg_r, lw["wo"], hr, ln_mix_g[l], ln_mix_b[l], alpha)
        if not last:
            q_m = _q_proj(z_m, zo[0], q_norm_g[l], lw["uq_n"], lw["uq_r"], ct_m, st_m)
            oa_m = _mla_attn(q_m, k_r, k_m, v_r, v_m, bsz, n_meta)
            ob_m = _swa_attn(sink, z_m, n_meta, z_r, z_m, zo[4], zo[5], bias_meta, type_meta, kv_meta_q, bsz, 1,
                             n_meta)
            mg_m = _gate_proj(oa_m, ob_m, z_m, zo[6], zo[7], lw["wa"], lw["wb"])
            hm = _out_ln(mg_m, lw["wo"], hm, ln_mix_g[l], ln_mix_b[l], alpha)

        if l % 2 == 0:
            w1, w3, w2 = (ffn_w1[l // 2].astype(BF16), ffn_w3[l // 2].astype(BF16), ffn_w2[l // 2].astype(BF16))
            hr, hr_b = _ffn_ln(hr, w1, w3, w2, ln_ffn_g[l], ln_ffn_b[l], alpha)
            if not last:
                hm, hm_b = _ffn_ln(hm, w1, w3, w2, ln_ffn_g[l], ln_ffn_b[l], alpha)
        else:
            hr = _moe_layer(hr, router_w[l // 2], router_b[l // 2], moe_w1[l // 2], moe_w3[l // 2], moe_w2[l // 2],
                            ln_ffn_g[l], ln_ffn_b[l], alpha)
            hr_b = hr.astype(BF16)
            if not last:
                hm = _moe_layer(hm, router_w[l // 2], router_b[l // 2], moe_w1[l // 2], moe_w3[l // 2],
                                moe_w2[l // 2], ln_ffn_g[l], ln_ffn_b[l], alpha)
                hm_b = hm.astype(BF16)
    return hr.reshape(bsz, seq, d)


def _moe_layer(h, router_w, router_b, w1, w3, w2, lg, lb, alpha):
    m, d = h.shape
    n_exp = router_w.shape[1]
    tm = _tile(m, 512)
    w_pad = jnp.zeros((d, V7X_LANES), F32).at[:, :n_exp].set(router_w)
    b_pad = jnp.zeros((1, V7X_LANES), F32).at[0, :n_exp].set(router_b)
    pk, cnt = _router(h, w_pad, b_pad, n_exp)
    counts = cnt[0, :n_exp].astype(jnp.int32)
    padded = (counts + tm - 1) // tm * tm
    pad_end = jnp.cumsum(padded)
    pad_start = pad_end - padded
    e1 = pk[:, _PK_E1].astype(jnp.int32)
    e2 = pk[:, _PK_E2].astype(jnp.int32)
    d1 = pad_start[e1] + pk[:, _PK_R1].astype(jnp.int32)
    d2 = pad_start[e2] + pk[:, _PK_R2].astype(jnp.int32)
    n_tiles = (m * TOP_K) // tm + n_exp
    n_slots = n_tiles * tm
    n_used = pad_end[-1] // tm
    tile_id = jnp.arange(n_tiles, dtype=jnp.int32)
    tile_valid = (tile_id < n_used).astype(jnp.int32)
    tile_src = jnp.minimum(tile_id, n_used - 1).astype(jnp.int32)
    tile_expert = jnp.minimum(jnp.searchsorted(pad_end, tile_src * tm, side="right"), n_exp - 1).astype(jnp.int32)
    pad_info = jnp.stack([jnp.append(pad_start + counts, pad_end[-1]),
                          jnp.append(padded - counts, n_slots - pad_end[-1])]).astype(jnp.int32)

    xs = _dispatch(h, d1, d2, pad_info, n_slots)
    ys = _moe_ffn(xs, w1.astype(BF16), w3.astype(BF16), w2.astype(BF16), tile_expert, tile_src, tile_valid, tm)
    return _combine_ln(h, pk, ys, d1, d2, lg, lb, alpha)
```

```python
import functools
import math

import jax
import jax.numpy as jnp
from jax import lax
from jax.experimental import pallas as pl
from jax.experimental.pallas import tpu as pltpu

MLA_HEADS = 16
MLA_NOPE_DIM = 128
MLA_ROPE_DIM = 64
MLA_V_DIM = 128
SWA_HEADS = 16
SWA_KV_HEADS = 4
SWA_HEAD_DIM = 128
WINDOW = 128
BLOCK = 128
N_BUCKETS = 32
MAX_DISTANCE = 128
TOP_K = 2
ROPE_THETA = 10000.0
LN_EPS = 1e-5
RMS_EPS = 1e-6
NEG = -1e30
LOG2E = math.log2(math.e)

V7X_LANES = 128
V7X_MXU_DIM = 256
V7X_VMEM_LIMIT_BYTES = 56 * 1024 * 1024

F32 = jnp.float32
BF16 = jnp.bfloat16


COMPILER_SCRATCH_BYTES = 8 * 1024 * 1024


def _cparams(semantics, vmem_bytes):
    limit = int(min(V7X_VMEM_LIMIT_BYTES, vmem_bytes + COMPILER_SCRATCH_BYTES))
    return pltpu.CompilerParams(dimension_semantics=semantics, vmem_limit_bytes=limit)


def _nbytes(shape, dtype):
    return math.prod(shape) * jnp.dtype(dtype).itemsize


def _dot(a, b):
    return jnp.dot(a, b, preferred_element_type=F32)


def _dot_t(a, b):
    return lax.dot_general(a, b, (((1,), (1,)), ((), ())), preferred_element_type=F32)


def _tile(n, pref):
    t = min(n, pref)
    while n % t:
        t //= 2
    return t


def _ln_rows(y, g, b):
    mu = jnp.mean(y, axis=-1, keepdims=True)
    d = y - mu
    var = jnp.mean(d * d, axis=-1, keepdims=True)
    return d * lax.rsqrt(var + LN_EPS) * g + b


def _embed_ln_kernel(x_ref, g_ref, b_ref, h_ref, hb_ref):
    y = _ln_rows(x_ref[...], g_ref[...], b_ref[...])
    h_ref[...] = y
    hb_ref[...] = y.astype(BF16)


def _embed_ln(x, g, b):
    m, d = x.shape
    tm = _tile(m, 512)
    row = pl.BlockSpec((tm, d), lambda i: (i, 0))
    vec = pl.BlockSpec((1, d), lambda i: (0, 0))
    return pl.pallas_call(
        _embed_ln_kernel,
        grid=(m // tm,),
        in_specs=[row, vec, vec],
        out_specs=[row, row],
        out_shape=[jax.ShapeDtypeStruct((m, d), F32), jax.ShapeDtypeStruct((m, d), BF16)],
        compiler_params=_cparams(("parallel",), 2 * tm * d * 10 + 4 * tm * d * 4),
        name="embed_ln",
    )(x, g.reshape(1, d), b.reshape(1, d))


def _mm_kernel(a_ref, w_ref, o_ref):
    o_ref[...] = _dot(a_ref[...], w_ref[...]).astype(o_ref.dtype)


def _matmul(a, w, out_dtype, tm_pref=1024, tn_pref=768):
    m, k = a.shape
    n = w.shape[1]
    tm = _tile(m, tm_pref)
    tn = next(t for t in (tn_pref, 512, 256, 128) if n % t == 0)
    vm = 2 * (_nbytes((tm, k), BF16) + _nbytes((k, tn), BF16) + _nbytes((tm, tn), out_dtype)) + _nbytes((tm, tn), F32)
    return pl.pallas_call(
        _mm_kernel,
        grid=(m // tm, n // tn),
        in_specs=[pl.BlockSpec((tm, k), lambda i, j: (i, 0)), pl.BlockSpec((k, tn), lambda i, j: (0, j))],
        out_specs=pl.BlockSpec((tm, tn), lambda i, j: (i, j)),
        out_shape=jax.ShapeDtypeStruct((m, n), out_dtype),
        compiler_params=_cparams(("parallel", "parallel"), vm),
        name="in_proj",
    )(a, w)


def _rope(x, ct, st):
    return x * ct + pltpu.roll(x, MLA_ROPE_DIM // 2, 1) * st


def _rms(c, g):
    return c * lax.rsqrt(jnp.mean(c * c, axis=-1, keepdims=True) + RMS_EPS) * g


def _q_proj_kernel(c_ref, g_ref, wn_ref, wr_ref, ct_ref, st_ref, o_ref, *, heads, scale):
    cb = _rms(c_ref[...].astype(F32), g_ref[...]).astype(BF16)
    qn = _dot(cb, wn_ref[...])
    qr = _dot(cb, wr_ref[...])
    ct = ct_ref[...]
    st = st_ref[...]
    w = MLA_NOPE_DIM + V7X_LANES
    for h in range(heads):
        o_ref[:, h * w:h * w + MLA_NOPE_DIM] = (qn[:, h * MLA_NOPE_DIM:(h + 1) * MLA_NOPE_DIM] * scale).astype(BF16)
        xr = qr[:, h * V7X_LANES:(h + 1) * V7X_LANES]
        o_ref[:, h * w + MLA_NOPE_DIM:(h + 1) * w] = (_rope(xr, ct, st) * scale).astype(BF16)


def _q_proj(z, cq_off, g, wn, wr, ct, st):
    m = z.shape[0]
    r = g.shape[0]
    heads = MLA_HEADS
    tm = _tile(math.gcd(m, ct.shape[0]), 512)
    npos = ct.shape[0] // tm
    assert cq_off % r == 0
    wout = heads * (MLA_NOPE_DIM + V7X_LANES)
    scale = (MLA_NOPE_DIM + MLA_ROPE_DIM) ** -0.5 * LOG2E
    vm = (2 * (_nbytes((tm, r), BF16) + _nbytes(wn.shape, BF16) + _nbytes(wr.shape, BF16) + 2 * tm * V7X_LANES * 4
               + _nbytes((tm, wout), BF16)) + 2 * _nbytes((tm, wn.shape[1]), F32) + _nbytes((tm, r), F32) * 2)
    tab = pl.BlockSpec((tm, V7X_LANES), lambda i: (i % npos, 0))
    return pl.pallas_call(
        functools.partial(_q_proj_kernel, heads=heads, scale=scale),
        grid=(m // tm,),
        in_specs=[pl.BlockSpec((tm, r), lambda i: (i, cq_off // r)),
                  pl.BlockSpec((1, r), lambda i: (0, 0)),
                  pl.BlockSpec(wn.shape, lambda i: (0, 0)),
                  pl.BlockSpec(wr.shape, lambda i: (0, 0)),
                  tab, tab],
        out_specs=pl.BlockSpec((tm, wout), lambda i: (i, 0)),
        out_shape=jax.ShapeDtypeStruct((m, wout), BF16),
        compiler_params=_cparams(("parallel",), vm),
        name="mla_q_proj",
    )(z, g.reshape(1, r), wn, wr, ct, st)


def _kv_proj_kernel(hb_ref, c_ref, g_ref, wk_ref, wv_ref, wkr_ref, ct_ref, st_ref, k_ref, v_ref, *, heads,
                    transpose_v):
    cb = _rms(c_ref[...].astype(F32), g_ref[...]).astype(BF16)
    kn = _dot(cb, wk_ref[...])
    if transpose_v:
        v_ref[...] = _dot_t(wv_ref[...], cb).astype(BF16)
    else:
        v_ref[...] = _dot(cb, wv_ref[...]).astype(BF16)
    kr = _rope(_dot(hb_ref[...], wkr_ref[...]), ct_ref[...], st_ref[...]).astype(BF16)
    w = MLA_NOPE_DIM + V7X_LANES
    for h in range(heads):
        k_ref[:, h * w:h * w + MLA_NOPE_DIM] = kn[:, h * MLA_NOPE_DIM:(h + 1) * MLA_NOPE_DIM].astype(BF16)
        k_ref[:, h * w + MLA_NOPE_DIM:(h + 1) * w] = kr


def _kv_proj(hb, z, ckv_off, g, wk, wv, wkr, ct, st, transpose_v):
    m, d = hb.shape
    r = g.shape[0]
    heads = MLA_HEADS
    tm = _tile(math.gcd(m, ct.shape[0]), 512)
    npos = ct.shape[0] // tm
    assert ckv_off % r == 0
    wk_out = heads * (MLA_NOPE_DIM + V7X_LANES)
    wv_out = heads * MLA_V_DIM
    if transpose_v:
        v_spec = pl.BlockSpec((wv_out, tm), lambda i: (0, i))
        v_shape = jax.ShapeDtypeStruct((wv_out, m), BF16)
    else:
        v_spec = pl.BlockSpec((tm, wv_out), lambda i: (i, 0))
        v_shape = jax.ShapeDtypeStruct((m, wv_out), BF16)
    vm = (2 * (_nbytes((tm, d), BF16) + _nbytes((tm, r), BF16) + _nbytes(wk.shape, BF16) + _nbytes(wv.shape, BF16)
               + _nbytes(wkr.shape, BF16) + 2 * tm * V7X_LANES * 4 + _nbytes((tm, wk_out), BF16)
               + _nbytes((tm, wv_out), BF16)) + 2 * _nbytes((tm, wv_out), F32) + _nbytes((tm, r), F32) * 2)
    tab = pl.BlockSpec((tm, V7X_LANES), lambda i: (i % npos, 0))
    return pl.pallas_call(
        functools.partial(_kv_proj_kernel, heads=heads, transpose_v=transpose_v),
        grid=(m // tm,),
        in_specs=[pl.BlockSpec((tm, d), lambda i: (i, 0)),
                  pl.BlockSpec((tm, r), lambda i: (i, ckv_off // r)),
                  pl.BlockSpec((1, r), lambda i: (0, 0)),
                  pl.BlockSpec(wk.shape, lambda i: (0, 0)),
                  pl.BlockSpec(wv.shape, lambda i: (0, 0)),
                  pl.BlockSpec(wkr.shape, lambda i: (0, 0)),
                  tab, tab],
        out_specs=[pl.BlockSpec((tm, wk_out), lambda i: (i, 0)), v_spec],
        out_shape=[jax.ShapeDtypeStruct((m, wk_out), BF16), v_shape],
        compiler_params=_cparams(("parallel",), vm),
        name="mla_kv_proj",
    )(hb, z, g.reshape(1, r), wk, wv, wkr, ct, st)


def _with_ones(v):
    return jnp.concatenate([v, jnp.ones(v.shape, v.dtype)], axis=1)


ONES_ROWS = 16


def _mla_attn_kernel(q_ref, k_ref, km_ref, vt_ref, vm_ref, o_ref, s_ref, m_ref, acc_ref, *, tk, n_meta, hb,
                     unroll):
    wq = MLA_NOPE_DIM + V7X_LANES
    dv = MLA_V_DIM
    tq = s_ref.shape[2]
    n_chunks = k_ref.shape[0] // tk
    pad = V7X_LANES - n_meta
    assert n_chunks % unroll == 0

    def q_of(j):
        q = q_ref[:, j * wq:(j + 1) * wq]
        if q.shape[0] < tq:
            q = jnp.concatenate([q, jnp.zeros((tq - q.shape[0], wq), BF16)], axis=0)
        return q

    def scores(j, c):
        start = pl.multiple_of(c * tk, tk)
        return _dot_t(k_ref[pl.ds(start, tk), j * wq:(j + 1) * wq], q_of(j))

    def accumulate(j, c, s):
        start = pl.multiple_of(c * tk, tk)
        m = m_ref[j, 0:1]
        m_new = jnp.maximum(m, jnp.max(s, axis=0, keepdims=True))
        vt = jnp.concatenate([vt_ref[j * dv:(j + 1) * dv, pl.ds(start, tk)], jnp.ones((ONES_ROWS, tk), BF16)], axis=0)
        acc_ref[j] = jnp.exp2(m - m_new) * acc_ref[j] + _dot(vt, jnp.exp2(s - m_new).astype(BF16))
        m_ref[j, 0:1] = m_new

    for j in range(hb):
        kmp = jnp.concatenate([km_ref[:, j * wq:(j + 1) * wq], jnp.zeros((pad, wq), BF16)], axis=0)
        vmp = jnp.concatenate([vm_ref[:, j * dv:(j + 1) * dv].astype(F32), jnp.zeros((pad, dv), F32)], axis=0)
        vt = jnp.concatenate([vmp.T.astype(BF16), jnp.ones((ONES_ROWS, V7X_LANES), BF16)], axis=0)
        s = _dot_t(kmp, q_of(j))
        row = lax.broadcasted_iota(jnp.int32, s.shape, 0)
        s = jnp.where(row < n_meta, s, NEG)
        m0 = jnp.max(s, axis=0, keepdims=True)
        m_ref[j, 0:1] = m0
        acc_ref[j] = _dot(vt, jnp.exp2(s - m0).astype(BF16))
        s_ref[j] = scores(j, 0)

    def group(c0, prefetch):
        cur = [s_ref[j] for j in range(hb)]
        for u in range(unroll):
            nxt = None
            if u < unroll - 1:
                nxt = [scores(j, c0 + u + 1) for j in range(hb)]
            elif prefetch:
                for j in range(hb):
                    s_ref[j] = scores(j, c0 + unroll)
            for j in range(hb):
                accumulate(j, c0 + u, cur[j])
            cur = nxt

    def body(i, carry):
        group(unroll * i, True)
        return carry

    lax.fori_loop(0, n_chunks // unroll - 1, body, 0)
    group(n_chunks - unroll, False)
    for j in range(hb):
        acc = acc_ref[j]
        o = (acc[:dv] / acc[dv:dv + 1]).T
        o_ref[:, j * dv:(j + 1) * dv] = o[:o_ref.shape[0]].astype(o_ref.dtype)


def _mla_attn(q, k, km, vt, vmeta, batch, n_meta):
    heads = MLA_HEADS
    hb = 2
    wq = MLA_NOPE_DIM + V7X_LANES
    dv = MLA_V_DIM
    rows_q = q.shape[0] // batch
    seq = k.shape[0] // batch
    tq = _tile(rows_q, 512)
    tk = _tile(seq, V7X_MXU_DIM)
    unroll = _tile(seq // tk, 16)
    nq = rows_q // tq
    tqp = max(tq, V7X_LANES)
    vm = (2 * hb * (_nbytes((tq, wq), BF16) + _nbytes((seq, wq), BF16) + _nbytes((seq, dv), BF16)
                    + _nbytes((tq, dv), BF16)) + hb * (6 * _nbytes((tk, tqp), F32) + 2 * _nbytes((2 * dv, tqp), F32)))
    return pl.pallas_call(
        functools.partial(_mla_attn_kernel, tk=tk, n_meta=n_meta, hb=hb, unroll=unroll),
        grid=(batch, heads // hb, nq),
        in_specs=[pl.BlockSpec((tq, hb * wq), lambda b, h, i: (b * nq + i, h)),
                  pl.BlockSpec((seq, hb * wq), lambda b, h, i: (b, h)),
                  pl.BlockSpec((n_meta, hb * wq), lambda b, h, i: (b, h)),
                  pl.BlockSpec((hb * dv, seq), lambda b, h, i: (h, b)),
                  pl.BlockSpec((n_meta, hb * dv), lambda b, h, i: (b, h))],
        out_specs=pl.BlockSpec((tq, hb * dv), lambda b, h, i: (b * nq + i, h)),
        out_shape=jax.ShapeDtypeStruct((q.shape[0], heads * dv), BF16),
        scratch_shapes=[pltpu.VMEM((hb, tk, tqp), F32), pltpu.VMEM((hb, 8, tqp), F32),
                        pltpu.VMEM((hb, dv + ONES_ROWS, tqp), F32)],
        compiler_params=_cparams(("parallel", "parallel", "parallel"), vm),
        name="mla_attn",
    )(q, k, km, vt, vmeta)


def _bias_kernel(rb_ref, bk_ref, vis_ref, o_ref, *, n_buckets):
    h = pl.program_id(1)
    bk = bk_ref[0]
    acc = jnp.zeros(bk.shape, F32)
    for j in range(n_buckets):
        acc = jnp.where(bk == j, rb_ref[j, h] * LOG2E, acc)
    o_ref[0, 0] = jnp.where(vis_ref[0] != 0, acc, NEG)


def _bias_tables(rel_bias, bucket, vis):
    t, r, s = bucket.shape
    heads = rel_bias.shape[1]
    blk = pl.BlockSpec((1, r, s), lambda i, h: (i, 0, 0))
    return pl.pallas_call(
        functools.partial(_bias_kernel, n_buckets=rel_bias.shape[0]),
        grid=(t, heads),
        in_specs=[pl.BlockSpec(memory_space=pltpu.SMEM), blk, blk],
        out_specs=pl.BlockSpec((1, 1, r, s), lambda i, h: (i, h, 0, 0)),
        out_shape=jax.ShapeDtypeStruct((t, heads, r, s), F32),
        compiler_params=_cparams(("parallel", "parallel"), 16 * r * s * 4),
        name="swa_bias_table",
    )(rel_bias, bucket, vis)


def _swa_kernel(sink_ref, q_ref, kp_ref, ko_ref, kn_ref, km_ref, vp_ref, vo_ref, vn_ref, vm_ref, bias_ref, o_ref,
                *, kv_heads, group, scale, n_meta):
    d = SWA_HEAD_DIM
    r = q_ref.shape[0]
    zpad = jnp.zeros((BLOCK - n_meta, d), BF16)

    for g in range(kv_heads):
        cols = slice(g * d, (g + 1) * d)
        kcat = jnp.concatenate([kp_ref[:, cols], ko_ref[:, cols], kn_ref[:, cols], km_ref[:, cols], zpad], axis=0)
        vcat = jnp.concatenate([vp_ref[:, cols], vo_ref[:, cols], vn_ref[:, cols], vm_ref[:, cols], zpad], axis=0)
        heads = range(g * group, (g + 1) * group)
        q = jnp.concatenate([q_ref[:, h * d:(h + 1) * d] for h in heads], axis=0)
        bias = jnp.concatenate([bias_ref[0, h] for h in heads], axis=0)
        sink = jnp.concatenate([jnp.full((r, 1), sink_ref[h] * LOG2E, F32) for h in heads], axis=0)
        s = _dot_t(q, kcat) * scale + bias
        m = jnp.maximum(jnp.max(s, axis=-1, keepdims=True), sink)
        acc = _dot(jnp.exp2(s - m).astype(BF16), _with_ones(vcat))
        o = acc[:, :d] / (acc[:, d:] + jnp.exp2(sink - m))
        for j, h in enumerate(heads):
            o_ref[:, h * d:(h + 1) * d] = o[j * r:(j + 1) * r].astype(o_ref.dtype)


def _swa_attn(sink, q_src, q_rows, kv_src, kv_meta, k_off, v_off, bias, type_of_block, kv_block_of, batch, n_grid_blk,
              n_meta):
    d = SWA_HEAD_DIM
    group = SWA_HEADS // SWA_KV_HEADS
    wq, wkv = SWA_HEADS * d, SWA_KV_HEADS * d
    nblk_kv = kv_src.shape[0] // batch // BLOCK
    s_keys = bias.shape[-1]
    assert k_off % wkv == 0 and v_off % wkv == 0
    kc, vc = k_off // wkv, v_off // wkv

    def kv_spec(which, col):
        return pl.BlockSpec((BLOCK, wkv), lambda b, i: (b * nblk_kv + kv_block_of(i)[which], col))

    def meta_spec(col):
        return pl.BlockSpec((n_meta, wkv), lambda b, i: (b, col))

    rows = group * q_rows
    vm = (2 * (2 * _nbytes((q_rows, wq), BF16) + 8 * _nbytes((BLOCK, wkv), BF16)
               + _nbytes((SWA_HEADS, q_rows, s_keys), F32))
          + SWA_KV_HEADS * (4 * _nbytes((rows, s_keys), F32) + _nbytes((rows, 2 * d), F32)))
    return pl.pallas_call(
        functools.partial(_swa_kernel, kv_heads=SWA_KV_HEADS, group=group, scale=d ** -0.5 * LOG2E, n_meta=n_meta),
        grid=(batch, n_grid_blk),
        in_specs=[pl.BlockSpec(memory_space=pltpu.SMEM),
                  pl.BlockSpec((q_rows, wq), lambda b, i: (b * n_grid_blk + i, 0)),
                  kv_spec(0, kc), kv_spec(1, kc), kv_spec(2, kc), meta_spec(kc),
                  kv_spec(0, vc), kv_spec(1, vc), kv_spec(2, vc), meta_spec(vc),
                  pl.BlockSpec((1, SWA_HEADS, q_rows, s_keys), lambda b, i: (type_of_block(i), 0, 0, 0))],
        out_specs=pl.BlockSpec((q_rows, wq), lambda b, i: (b * n_grid_blk + i, 0)),
        out_shape=jax.ShapeDtypeStruct((batch * n_grid_blk * q_rows, wq), BF16),
        compiler_params=_cparams(("parallel", "parallel"), vm),
        name="swa_attn",
    )(sink, q_src, kv_src, kv_src, kv_src, kv_meta, kv_src, kv_src, kv_src, kv_meta, bias)


def _gate_proj_kernel(oa_ref, ob_ref, ga_ref, gb_ref, wa_ref, wb_ref, o_ref):
    pa = _dot(oa_ref[...], wa_ref[...])
    pb = _dot(ob_ref[...], wb_ref[...])
    ga = jax.nn.sigmoid(ga_ref[...].astype(F32))
    gb = jax.nn.sigmoid(gb_ref[...].astype(F32))
    o_ref[...] = (ga * pa + gb * pb).astype(o_ref.dtype)


def _gate_proj(oa, ob, z, ga_off, gb_off, wa, wb):
    m, ka = oa.shape
    kb = ob.shape[1]
    n = wa.shape[1]
    tm = _tile(m, 1024)
    tn = _tile(n, 512)
    assert ga_off % tn == 0 and gb_off % tn == 0
    vm = (2 * (_nbytes((tm, ka + kb), BF16) + _nbytes((ka + kb, tn), BF16) + 3 * _nbytes((tm, tn), BF16))
          + 4 * _nbytes((tm, tn), F32))
    act = lambda k: pl.BlockSpec((tm, k), lambda i, j: (i, 0))
    wsp = lambda k: pl.BlockSpec((k, tn), lambda i, j: (0, j))
    return pl.pallas_call(
        _gate_proj_kernel,
        grid=(m // tm, n // tn),
        in_specs=[act(ka), act(kb),
                  pl.BlockSpec((tm, tn), lambda i, j: (i, ga_off // tn + j)),
                  pl.BlockSpec((tm, tn), lambda i, j: (i, gb_off // tn + j)),
                  wsp(ka), wsp(kb)],
        out_specs=pl.BlockSpec((tm, tn), lambda i, j: (i, j)),
        out_shape=jax.ShapeDtypeStruct((m, n), BF16),
        compiler_params=_cparams(("parallel", "parallel"), vm),
        name="gate_proj",
    )(oa, ob, z, z, wa, wb)


def _out_ln_kernel(m_ref, w_ref, h_ref, g_ref, b_ref, ho_ref, *, alpha):
    y = alpha * h_ref[...] + _dot(m_ref[...], w_ref[...])
    ho_ref[...] = _ln_rows(y, g_ref[...], b_ref[...])


def _out_ln(merged, w, h, g, b, alpha):
    m, d = h.shape
    tm = _tile(m, 512)
    row = pl.BlockSpec((tm, d), lambda i: (i, 0))
    vec = pl.BlockSpec((1, d), lambda i: (0, 0))
    vm = 2 * (_nbytes((tm, d), BF16) + _nbytes(w.shape, BF16) + 2 * _nbytes((tm, d), F32)) + 3 * _nbytes((tm, d), F32)
    return pl.pallas_call(
        functools.partial(_out_ln_kernel, alpha=alpha),
        grid=(m // tm,),
        in_specs=[row, pl.BlockSpec(w.shape, lambda i: (0, 0)), row, vec, vec],
        out_specs=row,
        out_shape=jax.ShapeDtypeStruct((m, d), F32),
        compiler_params=_cparams(("parallel",), vm),
        name="out_proj_ln",
    )(merged, w, h, g.reshape(1, d), b.reshape(1, d))


def _swiglu(a1, a3):
    return a1 * jax.nn.sigmoid(a1) * a3


def _ffn_ln_kernel(h_ref, w1_ref, w3_ref, w2_ref, lg_ref, lb_ref, ho_ref, hbo_ref, xb_ref, acc_ref, *, alpha):
    j = pl.program_id(1)

    @pl.when(j == 0)
    def _():
        xb_ref[...] = h_ref[...].astype(BF16)
        acc_ref[...] = jnp.zeros_like(acc_ref)

    xb = xb_ref[...]
    gact = _swiglu(_dot(xb, w1_ref[...]), _dot(xb, w3_ref[...])).astype(BF16)
    acc_ref[...] += _dot(gact, w2_ref[...])

    @pl.when(j == pl.num_programs(1) - 1)
    def _():
        y = _ln_rows(alpha * h_ref[...] + acc_ref[...], lg_ref[...], lb_ref[...])
        ho_ref[...] = y
        hbo_ref[...] = y.astype(BF16)


def _ffn_ln(h, w1, w3, w2, lg, lb, alpha):
    m, d = h.shape
    f = w1.shape[1]
    tm = _tile(m, 512)
    tf = _tile(f, 512)
    vm = (2 * (2 * _nbytes((tm, d), F32) + 3 * _nbytes((d, tf), BF16) + _nbytes((tm, d), BF16))
          + _nbytes((tm, d), BF16) + _nbytes((tm, d), F32) + 4 * _nbytes((tm, tf), F32) + 2 * _nbytes((tm, d), F32))
    row = pl.BlockSpec((tm, d), lambda i, j: (i, 0))
    vec = pl.BlockSpec((1, d), lambda i, j: (0, 0))
    wup = pl.BlockSpec((d, tf), lambda i, j: (0, j))
    return pl.pallas_call(
        functools.partial(_ffn_ln_kernel, alpha=alpha),
        grid=(m // tm, f // tf),
        in_specs=[row, wup, wup, pl.BlockSpec((tf, d), lambda i, j: (j, 0)), vec, vec],
        out_specs=[row, row],
        out_shape=[jax.ShapeDtypeStruct((m, d), F32), jax.ShapeDtypeStruct((m, d), BF16)],
        scratch_shapes=[pltpu.VMEM((tm, d), BF16), pltpu.VMEM((tm, d), F32)],
        compiler_params=_cparams(("parallel", "arbitrary"), vm),
        name="ffn_ln",
    )(h, w1, w3, w2, lg.reshape(1, d), lb.reshape(1, d))


_PK_E1, _PK_E2, _PK_G1, _PK_G2, _PK_R1, _PK_R2 = range(6)


def _router_kernel(h_ref, w_ref, b_ref, pk_ref, cnt_ref, carry_ref, *, n_exp):
    @pl.when(pl.program_id(0) == 0)
    def _():
        carry_ref[...] = jnp.zeros_like(carry_ref)

    logits = jnp.dot(h_ref[...], w_ref[...], preferred_element_type=F32, precision=lax.Precision.HIGHEST) + b_ref[...]
    tm, nl = logits.shape
    lane = lax.broadcasted_iota(jnp.int32, (tm, nl), 1).astype(F32)
    s = jnp.where(lane < n_exp, logits, -jnp.inf)
    m1 = jnp.max(s, axis=-1, keepdims=True)
    i1 = jnp.min(jnp.where(s == m1, lane, float(nl)), axis=-1, keepdims=True)
    s2 = jnp.where(lane == i1, -jnp.inf, s)
    m2 = jnp.max(s2, axis=-1, keepdims=True)
    i2 = jnp.min(jnp.where(s2 == m2, lane, float(nl)), axis=-1, keepdims=True)
    e = jnp.exp(m2 - m1)
    g1 = 1.0 / (1.0 + e)
    g2 = e / (1.0 + e)
    oh1 = lane == i1
    oh2 = lane == i2
    both = jnp.where(oh1 | oh2, 1.0, 0.0)
    r = lax.broadcasted_iota(jnp.int32, (tm, tm), 0)
    c = lax.broadcasted_iota(jnp.int32, (tm, tm), 1)
    tri = jnp.where(c < r, 1.0, 0.0).astype(BF16)
    prefix = _dot(tri, both.astype(BF16)) + carry_ref[...]
    r1 = jnp.sum(jnp.where(oh1, prefix, 0.0), axis=-1, keepdims=True)
    r2 = jnp.sum(jnp.where(oh2, prefix, 0.0), axis=-1, keepdims=True)
    carry_ref[...] += jnp.sum(both, axis=0, keepdims=True)
    cnt_ref[...] = carry_ref[...]
    pk = jnp.zeros((tm, nl), F32)
    for col, val in ((_PK_E1, i1), (_PK_E2, i2), (_PK_G1, g1), (_PK_G2, g2), (_PK_R1, r1), (_PK_R2, r2)):
        pk = jnp.where(lane == col, val, pk)
    pk_ref[...] = pk


def _router(h, w_pad, b_pad, n_exp):
    m, d = h.shape
    nl = w_pad.shape[1]
    tm = _tile(m, 512)
    vm = 2 * (_nbytes((tm, d), F32) + _nbytes((d, nl), F32) + _nbytes((tm, nl), F32)) + _nbytes((tm, tm), F32) * 3
    return pl.pallas_call(
        functools.partial(_router_kernel, n_exp=n_exp),
        grid=(m // tm,),
        in_specs=[pl.BlockSpec((tm, d), lambda i: (i, 0)), pl.BlockSpec((d, nl), lambda i: (0, 0)),
                  pl.BlockSpec((1, nl), lambda i: (0, 0))],
        out_specs=[pl.BlockSpec((tm, nl), lambda i: (i, 0)), pl.BlockSpec((1, nl), lambda i: (0, 0))],
        out_shape=[jax.ShapeDtypeStruct((m, nl), F32), jax.ShapeDtypeStruct((1, nl), F32)],
        scratch_shapes=[pltpu.VMEM((1, nl), F32)],
        compiler_params=_cparams(("arbitrary",), vm),
        name="moe_router",
    )(h, w_pad, b_pad)


def _row_copy(src_ref, src_row, dst_ref, dst_row, sem):
    return pltpu.make_async_copy(src_ref.at[pl.ds(src_row, 1)], dst_ref.at[pl.ds(dst_row, 1)], sem)


DMA_ISSUE_UNROLL = 8


def _dispatch_kernel(d1_ref, d2_ref, pad_ref, h_ref, zero_ref, xs_ref, sem, *, n_ranges):
    tb = h_ref.shape[0]

    def issue(t, carry):
        _row_copy(h_ref, t, xs_ref, d1_ref[0, 0, t], sem).start()
        _row_copy(h_ref, t, xs_ref, d2_ref[0, 0, t], sem).start()
        return carry

    lax.fori_loop(0, tb, issue, 0, unroll=DMA_ISSUE_UNROLL)

    @pl.when(pl.program_id(0) == 0)
    def _():
        for e in range(n_ranges):
            base = pad_ref[0, e]
            n = pad_ref[1, e]
            lax.fori_loop(0, n, lambda j, c: (_row_copy(zero_ref, 0, xs_ref, base + j, sem).start(), c)[1], 0)
            lax.fori_loop(0, n, lambda j, c: (_row_copy(zero_ref, 0, xs_ref, 0, sem).wait(), c)[1], 0)

    for _ in range(TOP_K):
        pltpu.make_async_copy(h_ref, xs_ref.at[pl.ds(0, tb)], sem).wait()


def _dispatch(h, d1, d2, pad_info, n_slots):
    m, d = h.shape
    tb = _tile(m, 512)
    nt = m // tb
    idx = pl.BlockSpec((1, 1, tb), lambda i: (i, 0, 0), memory_space=pltpu.SMEM)
    zero = jnp.zeros((8, d), F32)
    return pl.pallas_call(
        functools.partial(_dispatch_kernel, n_ranges=pad_info.shape[1]),
        grid=(nt,),
        in_specs=[idx, idx, pl.BlockSpec(memory_space=pltpu.SMEM), pl.BlockSpec((tb, d), lambda i: (i, 0)),
                  pl.BlockSpec((8, d), lambda i: (0, 0))],
        out_specs=pl.BlockSpec(memory_space=pl.ANY),
        out_shape=jax.ShapeDtypeStruct((n_slots, d), F32),
        scratch_shapes=[pltpu.SemaphoreType.DMA(())],
        compiler_params=_cparams(("arbitrary",), 4 * tb * d * 4),
        name="moe_dispatch",
    )(d1.reshape(nt, 1, tb), d2.reshape(nt, 1, tb), pad_info, h, zero)


def _moe_ffn_kernel(te_ref, ts_ref, tv_ref, x_ref, w1_ref, w3_ref, w2_ref, y_ref, xb_ref):
    i = pl.program_id(0)
    j = pl.program_id(1)

    @pl.when(tv_ref[i] != 0)
    def _():
        @pl.when(j == 0)
        def _():
            xb_ref[...] = x_ref[...].astype(BF16)
            y_ref[...] = jnp.zeros_like(y_ref)

        xb = xb_ref[...]
        gact = _swiglu(_dot(xb, w1_ref[0]), _dot(xb, w3_ref[0])).astype(BF16)
        y_ref[...] += _dot(gact, w2_ref[0])

    @pl.when((tv_ref[i] == 0) & (j == 0))
    def _():
        y_ref[...] = jnp.zeros_like(y_ref)


def _moe_ffn(xs, w1, w3, w2, tile_expert, tile_src, tile_valid, tm):
    n_slots, d = xs.shape
    f = w1.shape[2]
    tf = _tile(f, 512)
    nf = f // tf
    n_tiles = n_slots // tm

    def jj(i, j, tv):
        return jnp.where(tv[i] != 0, j, nf - 1)

    vm = (2 * (2 * _nbytes((tm, d), F32) + 3 * _nbytes((d, tf), BF16)) + _nbytes((tm, d), BF16)
          + 4 * _nbytes((tm, tf), F32) + _nbytes((tm, d), F32))
    grid_spec = pltpu.PrefetchScalarGridSpec(
        num_scalar_prefetch=3,
        grid=(n_tiles, nf),
        in_specs=[pl.BlockSpec((tm, d), lambda i, j, te, ts, tv: (ts[i], 0)),
                  pl.BlockSpec((1, d, tf), lambda i, j, te, ts, tv: (te[i], 0, jj(i, j, tv))),
                  pl.BlockSpec((1, d, tf), lambda i, j, te, ts, tv: (te[i], 0, jj(i, j, tv))),
                  pl.BlockSpec((1, tf, d), lambda i, j, te, ts, tv: (te[i], jj(i, j, tv), 0))],
        out_specs=pl.BlockSpec((tm, d), lambda i, j, te, ts, tv: (i, 0)),
        scratch_shapes=[pltpu.VMEM((tm, d), BF16)],
    )
    return pl.pallas_call(
        _moe_ffn_kernel,
        grid_spec=grid_spec,
        out_shape=jax.ShapeDtypeStruct((n_slots, d), F32),
        compiler_params=_cparams(("arbitrary", "arbitrary"), vm),
        name="moe_expert_ffn",
    )(tile_expert, tile_src, tile_valid, xs, w1, w3, w2)


def _combine_ln_kernel(d1_ref, d2_ref, d1n_ref, d2n_ref, pk_ref, h_ref, y_ref, lg_ref, lb_ref, o_ref, ybuf_ref, sems,
                       *, alpha, tb):
    step = pl.program_id(0)

    def gather(i1_ref, i2_ref, off, buf):
        def issue(t, carry):
            _row_copy(y_ref, i1_ref[0, 0, off + t], ybuf_ref.at[buf, 0], t, sems.at[buf]).start()
            _row_copy(y_ref, i2_ref[0, 0, off + t], ybuf_ref.at[buf, 1], t, sems.at[buf]).start()
            return carry

        lax.fori_loop(0, tb, issue, 0, unroll=DMA_ISSUE_UNROLL)

    def wait(buf):
        for k in range(TOP_K):
            pltpu.make_async_copy(y_ref.at[pl.ds(0, tb)], ybuf_ref.at[buf, k], sems.at[buf]).wait()

    def finish(buf, rows):
        pk = pk_ref[rows, :]
        f = pk[:, _PK_G1:_PK_G1 + 1] * ybuf_ref[buf, 0] + pk[:, _PK_G2:_PK_G2 + 1] * ybuf_ref[buf, 1]
        o_ref[rows, :] = _ln_rows(alpha * h_ref[rows, :] + f, lg_ref[...], lb_ref[...])

    @pl.when(step == 0)
    def _():
        gather(d1_ref, d2_ref, 0, 0)

    gather(d1_ref, d2_ref, tb, 1)
    wait(0)
    finish(0, slice(0, tb))

    @pl.when(step < pl.num_programs(0) - 1)
    def _():
        gather(d1n_ref, d2n_ref, 0, 0)

    wait(1)
    finish(1, slice(tb, 2 * tb))


def _combine_ln(h, pk, ys, d1, d2, lg, lb, alpha):
    m, d = h.shape
    tb = _tile(m // 2, 256)
    ns = m // (2 * tb)
    idx = pl.BlockSpec((1, 1, 2 * tb), lambda i: (i, 0, 0), memory_space=pltpu.SMEM)
    idx_next = pl.BlockSpec((1, 1, 2 * tb), lambda i: (jnp.minimum(i + 1, ns - 1), 0, 0), memory_space=pltpu.SMEM)
    row = pl.BlockSpec((2 * tb, d), lambda i: (i, 0))
    vec = pl.BlockSpec((1, d), lambda i: (0, 0))
    vm = (2 * (2 * _nbytes((2 * tb, d), F32) + _nbytes((2 * tb, pk.shape[1]), F32)) + 4 * _nbytes((tb, d), F32)
          + 4 * _nbytes((tb, d), F32))
    d1b, d2b = d1.reshape(ns, 1, 2 * tb), d2.reshape(ns, 1, 2 * tb)
    return pl.pallas_call(
        functools.partial(_combine_ln_kernel, alpha=alpha, tb=tb),
        grid=(ns,),
        in_specs=[idx, idx, idx_next, idx_next, pl.BlockSpec((2 * tb, pk.shape[1]), lambda i: (i, 0)), row,
                  pl.BlockSpec(memory_space=pl.ANY), vec, vec],
        out_specs=row,
        out_shape=jax.ShapeDtypeStruct((m, d), F32),
        scratch_shapes=[pltpu.VMEM((2, TOP_K, tb, d), F32), pltpu.SemaphoreType.DMA((2,))],
        compiler_params=_cparams(("arbitrary",), vm),
        name="moe_combine_ln",
    )(d1b, d2b, d1b, d2b, pk, h, ys, lg.reshape(1, d), lb.reshape(1, d))


def _rope_tables(n_tok):
    pos = jnp.arange(n_tok, dtype=F32)
    inv = ROPE_THETA ** (-jnp.arange(0, MLA_ROPE_DIM, 2, dtype=F32) / MLA_ROPE_DIM)
    ang = pos[:, None] * inv[None, :]
    cos, sin = jnp.cos(ang), jnp.sin(ang)
    zero = jnp.zeros_like(cos)
    return jnp.concatenate([cos, cos, zero, zero], -1), jnp.concatenate([-sin, sin, zero, zero], -1)


def _rel_bucket(rel):
    nb = N_BUCKETS // 2
    max_exact = nb // 2
    n = jnp.abs(rel)
    large = max_exact + (jnp.log(jnp.maximum(n, 1).astype(F32) / max_exact)
                         / math.log(MAX_DISTANCE / max_exact) * (nb - max_exact)).astype(jnp.int32)
    large = jnp.minimum(large, nb - 1)
    return jnp.where(rel > 0, nb, 0) + jnp.where(n < max_exact, n, large)


def _swa_index_tables(n_meta, n_real):
    nblk = n_real // BLOCK
    band = jnp.arange(3 * BLOCK)
    meta_pos = jnp.arange(n_meta)
    pad = BLOCK - n_meta

    def one(start, q_pos, first_block_only=False):
        r_key = start - BLOCK + band
        in_range = (r_key >= 0) & (r_key < n_real)
        if first_block_only:
            in_range = in_range & (r_key < BLOCK)
        k_pos = jnp.concatenate([n_meta + r_key, meta_pos, jnp.zeros((pad,), jnp.int32)])
        always = jnp.concatenate([jnp.zeros((3 * BLOCK,), bool), jnp.ones((n_meta,), bool), jnp.zeros((pad,), bool)])
        live = jnp.concatenate([in_range, jnp.ones((n_meta,), bool), jnp.zeros((pad,), bool)])
        rel = k_pos[None, :] - q_pos[:, None]
        vis = always[None, :] | (live[None, :] & (jnp.abs(rel) <= WINDOW))
        return _rel_bucket(rel), vis

    q_local = jnp.arange(BLOCK)
    tabs = [one(0, n_meta + q_local),
            one(BLOCK * min(1, nblk - 1), n_meta + BLOCK * min(1, nblk - 1) + q_local),
            one(BLOCK * (nblk - 1), n_meta + BLOCK * (nblk - 1) + q_local),
            one(0, jnp.where(q_local < n_meta, q_local, 0), first_block_only=True)]
    bucket = jnp.stack([t[0] for t in tabs]).astype(jnp.int32)
    vis = jnp.stack([t[1] for t in tabs]).astype(jnp.int32)
    return bucket, vis


def _layer_weights(l, w_in, w_uq, w_ukv, w_proj_a, w_proj_b, w_out, d_model, q_rank, kv_rank):
    hd = SWA_HEADS * SWA_HEAD_DIM
    kvd = SWA_KV_HEADS * SWA_HEAD_DIM
    splits = (q_rank, kv_rank, MLA_ROPE_DIM, hd, kvd, kvd, d_model, d_model)
    off = [0]
    for s in splits:
        off.append(off[-1] + s)
    col = lambda i: w_in[l][:, off[i]:off[i + 1]]
    order = (3, 6, 7, 1, 4, 5, 0)
    w_z = jnp.concatenate([col(i) for i in order], axis=1).astype(BF16)
    z_off = {}
    o = 0
    for i in order:
        z_off[i] = o
        o += splits[i]
    half = MLA_ROPE_DIM // 2
    kr = col(2)
    w_kr = jnp.concatenate([kr[:, :half], kr[:, half:], kr[:, :half], kr[:, half:]], axis=1).astype(BF16)
    uq = w_uq[l].reshape(q_rank, MLA_HEADS, MLA_NOPE_DIM + MLA_ROPE_DIM)
    uq_n = uq[:, :, :MLA_NOPE_DIM].reshape(q_rank, MLA_HEADS * MLA_NOPE_DIM).astype(BF16)
    r1 = uq[:, :, MLA_NOPE_DIM:MLA_NOPE_DIM + half]
    r2 = uq[:, :, MLA_NOPE_DIM + half:]
    uq_r = jnp.concatenate([r1, r2, r1, r2], axis=2).reshape(q_rank, MLA_HEADS * V7X_LANES).astype(BF16)
    ukv = w_ukv[l].reshape(kv_rank, MLA_HEADS, MLA_NOPE_DIM + MLA_V_DIM)
    uk = ukv[:, :, :MLA_NOPE_DIM].reshape(kv_rank, MLA_HEADS * MLA_NOPE_DIM).astype(BF16)
    uv = ukv[:, :, MLA_NOPE_DIM:].reshape(kv_rank, MLA_HEADS * MLA_V_DIM).astype(BF16)
    return dict(w_z=w_z, z_off=z_off, w_kr=w_kr, uq_n=uq_n, uq_r=uq_r, uk=uk, uv=uv, uv_t=uv.T,
                wa=w_proj_a[l].astype(BF16), wb=w_proj_b[l].astype(BF16), wo=w_out[l].astype(BF16))


def kernel(x, meta_tokens, emb_ln_g, emb_ln_b, rel_bias, w_in, q_norm_g, kv_norm_g, w_uq, w_ukv, sink_logits,
           w_proj_a, w_proj_b, w_out, ln_mix_g, ln_mix_b, ln_ffn_g, ln_ffn_b, ffn_w1, ffn_w3, ffn_w2, router_w,
           router_b, moe_w1, moe_w3, moe_w2):
    bsz, seq, d = x.shape
    n_meta = meta_tokens.shape[0]
    depth = w_in.shape[0]
    q_rank = q_norm_g.shape[1]
    kv_rank = kv_norm_g.shape[1]
    alpha = (2 * depth) ** 0.25
    nblk = seq // BLOCK
    assert seq % BLOCK == 0 and n_meta % 16 == 0 and n_meta <= BLOCK

    hr, hr_b = _embed_ln(x.reshape(bsz * seq, d), emb_ln_g, emb_ln_b)
    hm1, hm1_b = _embed_ln(meta_tokens.astype(x.dtype), emb_ln_g, emb_ln_b)
    hm, hm_b = jnp.tile(hm1, (bsz, 1)), jnp.tile(hm1_b, (bsz, 1))

    ct, st = _rope_tables(n_meta + seq)
    ct_r, st_r = ct[n_meta:], st[n_meta:]
    ct_m, st_m = jnp.tile(ct[:n_meta], (bsz, 1)), jnp.tile(st[:n_meta], (bsz, 1))
    bucket, vis = _swa_index_tables(n_meta, seq)
    bias_all = _bias_tables(rel_bias, bucket, vis)
    bias_real = bias_all[:3]
    bias_meta = bias_all[3:, :, :n_meta]

    last_blk = nblk - 1
    type_real = lambda i: jnp.where(i == 0, 0, jnp.where(i == last_blk, 2, 1))
    kv_real = lambda i: (jnp.maximum(i - 1, 0), i, jnp.minimum(i + 1, last_blk))
    type_meta = lambda i: 0
    kv_meta_q = lambda i: (0, 0, 0)

    for l in range(depth):
        last = l == depth - 1
        lw = _layer_weights(l, w_in, w_uq, w_ukv, w_proj_a, w_proj_b, w_out, d, q_rank, kv_rank)
        zo = lw["z_off"]
        sink = sink_logits[l]

        z_r = _matmul(hr_b, lw["w_z"], BF16)
        z_m = _matmul(hm_b, lw["w_z"], BF16)
        k_r, v_r = _kv_proj(hr_b, z_r, zo[1], kv_norm_g[l], lw["uk"], lw["uv_t"], lw["w_kr"], ct_r, st_r, True)
        k_m, v_m = _kv_proj(hm_b, z_m, zo[1], kv_norm_g[l], lw["uk"], lw["uv"], lw["w_kr"], ct_m, st_m, False)
        q_r = _q_proj(z_r, zo[0], q_norm_g[l], lw["uq_n"], lw["uq_r"], ct_r, st_r)
        oa_r = _mla_attn(q_r, k_r, k_m, v_r, v_m, bsz, n_meta)
        ob_r = _swa_attn(sink, z_r, BLOCK, z_r, z_m, zo[4], zo[5], bias_real, type_real, kv_real, bsz, nblk, n_meta)
        mg_r = _gate_proj(oa_r, ob_r, z_r, zo[6], zo[7], lw["wa"], lw["wb"])
        hr = _out_ln(mg_r, lw["wo"], hr, ln_mix_g[l], ln_mix_b[l], alpha)
        if not last:
            q_m = _q_proj(z_m, zo[0], q_norm_g[l], lw["uq_n"], lw["uq_r"], ct_m, st_m)
            oa_m = _mla_attn(q_m, k_r, k_m, v_r, v_m, bsz, n_meta)
            ob_m = _swa_attn(sink, z_m, n_meta, z_r, z_m, zo[4], zo[5], bias_meta, type_meta, kv_meta_q, bsz, 1,
                             n_meta)
            mg_m = _gate_proj(oa_m, ob_m, z_m, zo[6], zo[7], lw["wa"], lw["wb"])
            hm = _out_ln(mg_m, lw["wo"], hm, ln_mix_g[l], ln_mix_b[l], alpha)

        if l % 2 == 0:
            w1, w3, w2 = (ffn_w1[l // 2].astype(BF16), ffn_w3[l // 2].astype(BF16), ffn_w2[l // 2].astype(BF16))
            hr, hr_b = _ffn_ln(hr, w1, w3, w2, ln_ffn_g[l], ln_ffn_b[l], alpha)
            if not last:
                hm, hm_b = _ffn_ln(hm, w1, w3, w2, ln_ffn_g[l], ln_ffn_b[l], alpha)
        else:
            hr = _moe_layer(hr, router_w[l // 2], router_b[l // 2], moe_w1[l // 2], moe_w3[l // 2], moe_w2[l // 2],
                            ln_ffn_g[l], ln_ffn_b[l], alpha)
            hr_b = hr.astype(BF16)
            if not last:
                hm = _moe_layer(hm, router_w[l // 2], router_b[l // 2], moe_w1[l // 2], moe_w3[l // 2],
                                moe_w2[l // 2], ln_ffn_g[l], ln_ffn_b[l], alpha)
                hm_b = hm.astype(BF16)
    return hr.reshape(bsz, seq, d)


def _moe_layer(h, router_w, router_b, w1, w3, w2, lg, lb, alpha):
    m, d = h.shape
    n_exp = router_w.shape[1]
    tm = _tile(m, 512)
    w_pad = jnp.zeros((d, V7X_LANES), F32).at[:, :n_exp].set(router_w)
    b_pad = jnp.zeros((1, V7X_LANES), F32).at[0, :n_exp].set(router_b)
    pk, cnt = _router(h, w_pad, b_pad, n_exp)
    counts = cnt[0, :n_exp].astype(jnp.int32)
    padded = (counts + tm - 1) // tm * tm
    pad_end = jnp.cumsum(padded)
    pad_start = pad_end - padded
    e1 = pk[:, _PK_E1].astype(jnp.int32)
    e2 = pk[:, _PK_E2].astype(jnp.int32)
    d1 = pad_start[e1] + pk[:, _PK_R1].astype(jnp.int32)
    d2 = pad_start[e2] + pk[:, _PK_R2].astype(jnp.int32)
    n_tiles = (m * TOP_K) // tm + n_exp
    n_slots = n_tiles * tm
    n_used = pad_end[-1] // tm
    tile_id = jnp.arange(n_tiles, dtype=jnp.int32)
    tile_valid = (tile_id < n_used).astype(jnp.int32)
    tile_src = jnp.minimum(tile_id, n_used - 1).astype(jnp.int32)
    tile_expert = jnp.minimum(jnp.searchsorted(pad_end, tile_src * tm, side="right"), n_exp - 1).astype(jnp.int32)
    pad_info = jnp.stack([jnp.append(pad_start + counts, pad_end[-1]),
                          jnp.append(padded - counts, n_slots - pad_end[-1])]).astype(jnp.int32)

    xs = _dispatch(h, d1, d2, pad_info, n_slots)
    ys = _moe_ffn(xs, w1.astype(BF16), w3.astype(BF16), w2.astype(BF16), tile_expert, tile_src, tile_valid, tm)
    return _combine_ln(h, pk, ys, d1, d2, lg, lb, alpha)
```

```python
import functools
import math

import jax
import jax.numpy as jnp
from jax import lax
from jax.experimental import pallas as pl
from jax.experimental.pallas import tpu as pltpu

MLA_HEADS = 16
MLA_NOPE_DIM = 128
MLA_ROPE_DIM = 64
MLA_V_DIM = 128
SWA_HEADS = 16
SWA_KV_HEADS = 4
SWA_HEAD_DIM = 128
WINDOW = 128
BLOCK = 128
N_BUCKETS = 32
MAX_DISTANCE = 128
TOP_K = 2
ROPE_THETA = 10000.0
LN_EPS = 1e-5
RMS_EPS = 1e-6
NEG = -1e30
LOG2E = math.log2(math.e)

V7X_LANES = 128
V7X_MXU_DIM = 256
V7X_VMEM_LIMIT_BYTES = 56 * 1024 * 1024

F32 = jnp.float32
BF16 = jnp.bfloat16


COMPILER_SCRATCH_BYTES = 8 * 1024 * 1024


def _cparams(semantics, vmem_bytes):
    limit = int(min(V7X_VMEM_LIMIT_BYTES, vmem_bytes + COMPILER_SCRATCH_BYTES))
    return pltpu.CompilerParams(dimension_semantics=semantics, vmem_limit_bytes=limit)


def _nbytes(shape, dtype):
    return math.prod(shape) * jnp.dtype(dtype).itemsize


def _dot(a, b):
    return jnp.dot(a, b, preferred_element_type=F32)


def _dot_t(a, b):
    return lax.dot_general(a, b, (((1,), (1,)), ((), ())), preferred_element_type=F32)


def _tile(n, pref):
    t = min(n, pref)
    while n % t:
        t //= 2
    return t


def _ln_rows(y, g, b):
    mu = jnp.mean(y, axis=-1, keepdims=True)
    d = y - mu
    var = jnp.mean(d * d, axis=-1, keepdims=True)
    return d * lax.rsqrt(var + LN_EPS) * g + b


def _embed_ln_kernel(x_ref, g_ref, b_ref, h_ref, hb_ref):
    y = _ln_rows(x_ref[...], g_ref[...], b_ref[...])
    h_ref[...] = y
    hb_ref[...] = y.astype(BF16)


def _embed_ln(x, g, b):
    m, d = x.shape
    tm = _tile(m, 512)
    row = pl.BlockSpec((tm, d), lambda i: (i, 0))
    vec = pl.BlockSpec((1, d), lambda i: (0, 0))
    return pl.pallas_call(
        _embed_ln_kernel,
        grid=(m // tm,),
        in_specs=[row, vec, vec],
        out_specs=[row, row],
        out_shape=[jax.ShapeDtypeStruct((m, d), F32), jax.ShapeDtypeStruct((m, d), BF16)],
        compiler_params=_cparams(("parallel",), 2 * tm * d * 10 + 4 * tm * d * 4),
        name="embed_ln",
    )(x, g.reshape(1, d), b.reshape(1, d))


def _mm_kernel(a_ref, w_ref, o_ref):
    o_ref[...] = _dot(a_ref[...], w_ref[...]).astype(o_ref.dtype)


def _matmul(a, w, out_dtype, tm_pref=1024, tn_pref=768):
    m, k = a.shape
    n = w.shape[1]
    tm = _tile(m, tm_pref)
    tn = next(t for t in (tn_pref, 512, 256, 128) if n % t == 0)
    vm = 2 * (_nbytes((tm, k), BF16) + _nbytes((k, tn), BF16) + _nbytes((tm, tn), out_dtype)) + _nbytes((tm, tn), F32)
    return pl.pallas_call(
        _mm_kernel,
        grid=(m // tm, n // tn),
        in_specs=[pl.BlockSpec((tm, k), lambda i, j: (i, 0)), pl.BlockSpec((k, tn), lambda i, j: (0, j))],
        out_specs=pl.BlockSpec((tm, tn), lambda i, j: (i, j)),
        out_shape=jax.ShapeDtypeStruct((m, n), out_dtype),
        compiler_params=_cparams(("parallel", "parallel"), vm),
        name="in_proj",
    )(a, w)


def _rope(x, ct, st):
    return x * ct + pltpu.roll(x, MLA_ROPE_DIM // 2, 1) * st


def _rms(c, g):
    return c * lax.rsqrt(jnp.mean(c * c, axis=-1, keepdims=True) + RMS_EPS) * g


SHIFT_LANE = MLA_ROPE_DIM
BOUND_MARGIN = 1.0 + 2.0 ** -7


def _sq_norm(*parts):
    sq = sum(jnp.square(p.astype(F32)) for p in parts)
    return jnp.sum(sq, axis=-1, keepdims=True)


def _q_proj_kernel(c_ref, g_ref, wn_ref, wr_ref, ct_ref, st_ref, km_ref, hsum_ref, hput_ref, o_ref, *, heads, scale):
    cb = _rms(c_ref[...].astype(F32), g_ref[...]).astype(BF16)
    ct = ct_ref[...]
    st = st_ref[...]
    qn = (_dot(cb, wn_ref[...]) * scale).astype(BF16)
    qr_f = _dot(cb, wr_ref[...])
    qr = jnp.concatenate([(_rope(qr_f[:, h * V7X_LANES:(h + 1) * V7X_LANES], ct, st) * scale).astype(BF16)
                          for h in range(heads)], axis=1)
    sq = (jnp.square(qn.astype(F32)) + jnp.square(qr.astype(F32))).astype(BF16)
    bound = jnp.sqrt(_dot(sq, hsum_ref[...]) * km_ref[...]) * BOUND_MARGIN
    shift = _dot((-bound).astype(BF16), hput_ref[...])
    qr = (qr.astype(F32) + shift).astype(BF16)
    w = MLA_NOPE_DIM + V7X_LANES
    for h in range(heads):
        o_ref[:, h * w:h * w + MLA_NOPE_DIM] = qn[:, h * MLA_NOPE_DIM:(h + 1) * MLA_NOPE_DIM]
        o_ref[:, h * w + MLA_NOPE_DIM:(h + 1) * w] = qr[:, h * V7X_LANES:(h + 1) * V7X_LANES]


def _q_proj(z, cq_off, g, wn, wr, ct, st, kmax_rows):
    m = z.shape[0]
    r = g.shape[0]
    heads = MLA_HEADS
    tm = _tile(math.gcd(m, ct.shape[0]), 512)
    npos = ct.shape[0] // tm
    assert cq_off % r == 0
    wout = heads * (MLA_NOPE_DIM + V7X_LANES)
    scale = (MLA_NOPE_DIM + MLA_ROPE_DIM) ** -0.5 * LOG2E
    vm = (2 * (_nbytes((tm, r), BF16) + _nbytes(wn.shape, BF16) + _nbytes(wr.shape, BF16) + 2 * tm * V7X_LANES * 4
               + _nbytes((tm, wout), BF16)) + 2 * _nbytes((tm, wn.shape[1]), F32) + _nbytes((tm, r), F32) * 2)
    tab = pl.BlockSpec((tm, V7X_LANES), lambda i: (i % npos, 0))
    assert MLA_NOPE_DIM == V7X_LANES and heads <= V7X_LANES
    chan = jnp.arange(heads * V7X_LANES)
    head_id = jnp.arange(V7X_LANES)
    hsum = (chan[:, None] // V7X_LANES == head_id[None, :]).astype(BF16)
    hput = (head_id[:, None] * V7X_LANES + SHIFT_LANE == chan[None, :]).astype(BF16)
    const = lambda a: pl.BlockSpec(a.shape, lambda i: (0, 0))
    return pl.pallas_call(
        functools.partial(_q_proj_kernel, heads=heads, scale=scale),
        grid=(m // tm,),
        in_specs=[pl.BlockSpec((tm, r), lambda i: (i, cq_off // r)),
                  pl.BlockSpec((1, r), lambda i: (0, 0)),
                  const(wn), const(wr),
                  tab, tab, pl.BlockSpec((tm, V7X_LANES), lambda i: (i, 0)), const(hsum), const(hput)],
        out_specs=pl.BlockSpec((tm, wout), lambda i: (i, 0)),
        out_shape=jax.ShapeDtypeStruct((m, wout), BF16),
        compiler_params=_cparams(("parallel",), vm),
        name="mla_q_proj",
    )(z, g.reshape(1, r), wn, wr, ct, st, kmax_rows, hsum, hput)


def _kv_proj_kernel(hb_ref, c_ref, g_ref, wk_ref, wv_ref, wkr_ref, ct_ref, st_ref, k_ref, v_ref, ksq_ref, *, heads,
                    transpose_v):
    cb = _rms(c_ref[...].astype(F32), g_ref[...]).astype(BF16)
    kn = _dot(cb, wk_ref[...])
    if transpose_v:
        v_ref[...] = _dot_t(wv_ref[...], cb).astype(BF16)
    else:
        v_ref[...] = _dot(cb, wv_ref[...]).astype(BF16)
    kr = _rope(_dot(hb_ref[...], wkr_ref[...]), ct_ref[...], st_ref[...]).astype(BF16)
    lane = lax.broadcasted_iota(jnp.int32, kr.shape, 1)
    kr_one = jnp.where(lane == SHIFT_LANE, 1.0, kr).astype(BF16)
    kr_sq = _sq_norm(kr)
    ksq = jnp.zeros(kr.shape, F32)
    w = MLA_NOPE_DIM + V7X_LANES
    for h in range(heads):
        kn_h = kn[:, h * MLA_NOPE_DIM:(h + 1) * MLA_NOPE_DIM].astype(BF16)
        k_ref[:, h * w:h * w + MLA_NOPE_DIM] = kn_h
        k_ref[:, h * w + MLA_NOPE_DIM:(h + 1) * w] = kr_one
        ksq = jnp.where(lane == h, _sq_norm(kn_h) + kr_sq, ksq)
    ksq_ref[...] = ksq


def _kv_proj(hb, z, ckv_off, g, wk, wv, wkr, ct, st, transpose_v):
    m, d = hb.shape
    r = g.shape[0]
    heads = MLA_HEADS
    tm = _tile(math.gcd(m, ct.shape[0]), 512)
    npos = ct.shape[0] // tm
    assert ckv_off % r == 0
    wk_out = heads * (MLA_NOPE_DIM + V7X_LANES)
    wv_out = heads * MLA_V_DIM
    if transpose_v:
        v_spec = pl.BlockSpec((wv_out, tm), lambda i: (0, i))
        v_shape = jax.ShapeDtypeStruct((wv_out, m), BF16)
    else:
        v_spec = pl.BlockSpec((tm, wv_out), lambda i: (i, 0))
        v_shape = jax.ShapeDtypeStruct((m, wv_out), BF16)
    vm = (2 * (_nbytes((tm, d), BF16) + _nbytes((tm, r), BF16) + _nbytes(wk.shape, BF16) + _nbytes(wv.shape, BF16)
               + _nbytes(wkr.shape, BF16) + 2 * tm * V7X_LANES * 4 + _nbytes((tm, wk_out), BF16)
               + _nbytes((tm, wv_out), BF16)) + 2 * _nbytes((tm, wv_out), F32) + _nbytes((tm, r), F32) * 2)
    tab = pl.BlockSpec((tm, V7X_LANES), lambda i: (i % npos, 0))
    return pl.pallas_call(
        functools.partial(_kv_proj_kernel, heads=heads, transpose_v=transpose_v),
        grid=(m // tm,),
        in_specs=[pl.BlockSpec((tm, d), lambda i: (i, 0)),
                  pl.BlockSpec((tm, r), lambda i: (i, ckv_off // r)),
                  pl.BlockSpec((1, r), lambda i: (0, 0)),
                  pl.BlockSpec(wk.shape, lambda i: (0, 0)),
                  pl.BlockSpec(wv.shape, lambda i: (0, 0)),
                  pl.BlockSpec(wkr.shape, lambda i: (0, 0)),
                  tab, tab],
        out_specs=[pl.BlockSpec((tm, wk_out), lambda i: (i, 0)), v_spec,
                   pl.BlockSpec((tm, V7X_LANES), lambda i: (i, 0))],
        out_shape=[jax.ShapeDtypeStruct((m, wk_out), BF16), v_shape, jax.ShapeDtypeStruct((m, V7X_LANES), F32)],
        compiler_params=_cparams(("parallel",), vm),
        name="mla_kv_proj",
    )(hb, z, g.reshape(1, r), wk, wv, wkr, ct, st)


def _with_ones(v):
    return jnp.concatenate([v, jnp.ones(v.shape, v.dtype)], axis=1)


ONES_ROWS = 16


UNDERFLOW_GUARD = 2.0 ** -100


def _mla_attn_kernel(q_ref, k_ref, km_ref, vt_ref, vm_ref, o_ref, m_ref, acc_ref, *, tk, n_meta, hb):
    wq = MLA_NOPE_DIM + V7X_LANES
    dv = MLA_V_DIM
    tq = acc_ref.shape[2]
    n_chunks = k_ref.shape[0] // tk
    pad = V7X_LANES - n_meta

    def q_of(j, shifted):
        q = q_ref[:, j * wq:(j + 1) * wq]
        if not shifted:
            lane = lax.broadcasted_iota(jnp.int32, q.shape, 1)
            q = jnp.where(lane == MLA_NOPE_DIM + SHIFT_LANE, 0.0, q).astype(BF16)
        if q.shape[0] < tq:
            q = jnp.concatenate([q, jnp.zeros((tq - q.shape[0], wq), BF16)], axis=0)
        return q

    def scores(j, c, shifted):
        start = pl.multiple_of(c * tk, tk)
        return _dot_t(k_ref[pl.ds(start, tk), j * wq:(j + 1) * wq], q_of(j, shifted))

    def vt_of(j, c):
        start = pl.multiple_of(c * tk, tk)
        return jnp.concatenate([vt_ref[j * dv:(j + 1) * dv, pl.ds(start, tk)], jnp.ones((ONES_ROWS, tk), BF16)], axis=0)

    def meta_scores(j, shifted):
        kmp = jnp.concatenate([km_ref[:, j * wq:(j + 1) * wq], jnp.zeros((pad, wq), BF16)], axis=0)
        s = _dot_t(kmp, q_of(j, shifted))
        row = lax.broadcasted_iota(jnp.int32, s.shape, 0)
        return jnp.where(row < n_meta, s, NEG)

    def meta_vt(j):
        vmp = jnp.concatenate([vm_ref[:, j * dv:(j + 1) * dv].astype(F32), jnp.zeros((pad, dv), F32)], axis=0)
        return jnp.concatenate([vmp.T.astype(BF16), jnp.ones((ONES_ROWS, V7X_LANES), BF16)], axis=0)

    def store(j, acc):
        o = (acc[:dv] / acc[dv:dv + 1]).T
        o_ref[:, j * dv:(j + 1) * dv] = o[:o_ref.shape[0]].astype(o_ref.dtype)

    acc = [_dot(meta_vt(j), jnp.exp2(meta_scores(j, True)).astype(BF16)) for j in range(hb)]
    s = [scores(j, 0, True) for j in range(hb)]
    for c in range(n_chunks):
        for j in range(hb):
            p = jnp.exp2(s[j]).astype(BF16)
            if c + 1 < n_chunks:
                s[j] = scores(j, c + 1, True)
            acc[j] = acc[j] + _dot(vt_of(j, c), p)
    for j in range(hb):
        store(j, acc[j])

    def online_head(j):
        def accumulate(c, sc):
            m = m_ref[j, 0:1]
            m_new = jnp.maximum(m, jnp.max(sc, axis=0, keepdims=True))
            acc_ref[j] = jnp.exp2(m - m_new) * acc_ref[j] + _dot(vt_of(j, c), jnp.exp2(sc - m_new).astype(BF16))
            m_ref[j, 0:1] = m_new

        sm = meta_scores(j, False)
        m0 = jnp.max(sm, axis=0, keepdims=True)
        m_ref[j, 0:1] = m0
        acc_ref[j] = _dot(meta_vt(j), jnp.exp2(sm - m0).astype(BF16))

        def body(c, carry):
            accumulate(c, scores(j, c, False))
            return carry

        lax.fori_loop(0, n_chunks, body, 0)
        store(j, acc_ref[j])

    for j in range(hb):
        denom_ok = jnp.min(acc[j][dv:dv + 1]) >= UNDERFLOW_GUARD

        @pl.when(jnp.logical_not(denom_ok))
        def _():
            online_head(j)


def _mla_attn(q, k, km, vt, vmeta, batch, n_meta):
    heads = MLA_HEADS
    hb = 2
    wq = MLA_NOPE_DIM + V7X_LANES
    dv = MLA_V_DIM
    rows_q = q.shape[0] // batch
    seq = k.shape[0] // batch
    tq = _tile(rows_q, 512)
    tk = _tile(seq, 4 * V7X_MXU_DIM)
    nq = rows_q // tq
    tqp = max(tq, V7X_LANES)
    vm = (2 * hb * (_nbytes((tq, wq), BF16) + _nbytes((seq, wq), BF16) + _nbytes((seq, dv), BF16)
                    + _nbytes((tq, dv), BF16)) + hb * (6 * _nbytes((tk, tqp), F32) + 2 * _nbytes((2 * dv, tqp), F32)))
    return pl.pallas_call(
        functools.partial(_mla_attn_kernel, tk=tk, n_meta=n_meta, hb=hb),
        grid=(batch, heads // hb, nq),
        in_specs=[pl.BlockSpec((tq, hb * wq), lambda b, h, i: (b * nq + i, h)),
                  pl.BlockSpec((seq, hb * wq), lambda b, h, i: (b, h)),
                  pl.BlockSpec((n_meta, hb * wq), lambda b, h, i: (b, h)),
                  pl.BlockSpec((hb * dv, seq), lambda b, h, i: (h, b)),
                  pl.BlockSpec((n_meta, hb * dv), lambda b, h, i: (b, h))],
        out_specs=pl.BlockSpec((tq, hb * dv), lambda b, h, i: (b * nq + i, h)),
        out_shape=jax.ShapeDtypeStruct((q.shape[0], heads * dv), BF16),
        scratch_shapes=[pltpu.VMEM((hb, 8, tqp), F32), pltpu.VMEM((hb, dv + ONES_ROWS, tqp), F32)],
        compiler_params=_cparams(("parallel", "parallel", "parallel"), vm),
        name="mla_attn",
    )(q, k, km, vt, vmeta)


def _bias_kernel(rb_ref, bk_ref, vis_ref, o_ref, *, n_buckets):
    h = pl.program_id(1)
    bk = bk_ref[0]
    acc = jnp.zeros(bk.shape, F32)
    for j in range(n_buckets):
        acc = jnp.where(bk == j, rb_ref[j, h] * LOG2E, acc)
    o_ref[0, 0] = jnp.where(vis_ref[0] != 0, acc, NEG)


def _bias_tables(rel_bias, bucket, vis):
    t, r, s = bucket.shape
    heads = rel_bias.shape[1]
    blk = pl.BlockSpec((1, r, s), lambda i, h: (i, 0, 0))
    return pl.pallas_call(
        functools.partial(_bias_kernel, n_buckets=rel_bias.shape[0]),
        grid=(t, heads),
        in_specs=[pl.BlockSpec(memory_space=pltpu.SMEM), blk, blk],
        out_specs=pl.BlockSpec((1, 1, r, s), lambda i, h: (i, h, 0, 0)),
        out_shape=jax.ShapeDtypeStruct((t, heads, r, s), F32),
        compiler_params=_cparams(("parallel", "parallel"), 16 * r * s * 4),
        name="swa_bias_table",
    )(rel_bias, bucket, vis)


def _swa_kernel(sink_ref, q_ref, kp_ref, ko_ref, kn_ref, km_ref, vp_ref, vo_ref, vn_ref, vm_ref, bias_ref, o_ref,
                *, kv_heads, group, scale, n_meta):
    d = SWA_HEAD_DIM
    r = q_ref.shape[0]
    zpad = jnp.zeros((BLOCK - n_meta, d), BF16)

    for g in range(kv_heads):
        cols = slice(g * d, (g + 1) * d)
        kcat = jnp.concatenate([kp_ref[:, cols], ko_ref[:, cols], kn_ref[:, cols], km_ref[:, cols], zpad], axis=0)
        vcat = jnp.concatenate([vp_ref[:, cols], vo_ref[:, cols], vn_ref[:, cols], vm_ref[:, cols], zpad], axis=0)
        heads = range(g * group, (g + 1) * group)
        q = jnp.concatenate([q_ref[:, h * d:(h + 1) * d] for h in heads], axis=0)
        bias = jnp.concatenate([bias_ref[0, h] for h in heads], axis=0)
        sink = jnp.concatenate([jnp.full((r, 1), sink_ref[h] * LOG2E, F32) for h in heads], axis=0)
        s = _dot_t(q, kcat) * scale + bias
        m = jnp.maximum(jnp.max(s, axis=-1, keepdims=True), sink)
        acc = _dot(jnp.exp2(s - m).astype(BF16), _with_ones(vcat))
        o = acc[:, :d] / (acc[:, d:] + jnp.exp2(sink - m))
        for j, h in enumerate(heads):
            o_ref[:, h * d:(h + 1) * d] = o[j * r:(j + 1) * r].astype(o_ref.dtype)


def _swa_attn(sink, q_src, q_rows, kv_src, kv_meta, k_off, v_off, bias, type_of_block, kv_block_of, batch, n_grid_blk,
              n_meta):
    d = SWA_HEAD_DIM
    group = SWA_HEADS // SWA_KV_HEADS
    wq, wkv = SWA_HEADS * d, SWA_KV_HEADS * d
    nblk_kv = kv_src.shape[0] // batch // BLOCK
    s_keys = bias.shape[-1]
    assert k_off % wkv == 0 and v_off % wkv == 0
    kc, vc = k_off // wkv, v_off // wkv

    def kv_spec(which, col):
        return pl.BlockSpec((BLOCK, wkv), lambda b, i: (b * nblk_kv + kv_block_of(i)[which], col))

    def meta_spec(col):
        return pl.BlockSpec((n_meta, wkv), lambda b, i: (b, col))

    rows = group * q_rows
    vm = (2 * (2 * _nbytes((q_rows, wq), BF16) + 8 * _nbytes((BLOCK, wkv), BF16)
               + _nbytes((SWA_HEADS, q_rows, s_keys), F32))
          + SWA_KV_HEADS * (4 * _nbytes((rows, s_keys), F32) + _nbytes((rows, 2 * d), F32)))
    return pl.pallas_call(
        functools.partial(_swa_kernel, kv_heads=SWA_KV_HEADS, group=group, scale=d ** -0.5 * LOG2E, n_meta=n_meta),
        grid=(batch, n_grid_blk),
        in_specs=[pl.BlockSpec(memory_space=pltpu.SMEM),
                  pl.BlockSpec((q_rows, wq), lambda b, i: (b * n_grid_blk + i, 0)),
                  kv_spec(0, kc), kv_spec(1, kc), kv_spec(2, kc), meta_spec(kc),
                  kv_spec(0, vc), kv_spec(1, vc), kv_spec(2, vc), meta_spec(vc),
                  pl.BlockSpec((1, SWA_HEADS, q_rows, s_keys), lambda b, i: (type_of_block(i), 0, 0, 0))],
        out_specs=pl.BlockSpec((q_rows, wq), lambda b, i: (b * n_grid_blk + i, 0)),
        out_shape=jax.ShapeDtypeStruct((batch * n_grid_blk * q_rows, wq), BF16),
        compiler_params=_cparams(("parallel", "parallel"), vm),
        name="swa_attn",
    )(sink, q_src, kv_src, kv_src, kv_src, kv_meta, kv_src, kv_src, kv_src, kv_meta, bias)


def _gate_proj_kernel(oa_ref, ob_ref, ga_ref, gb_ref, wa_ref, wb_ref, o_ref):
    pa = _dot(oa_ref[...], wa_ref[...])
    pb = _dot(ob_ref[...], wb_ref[...])
    ga = jax.nn.sigmoid(ga_ref[...].astype(F32))
    gb = jax.nn.sigmoid(gb_ref[...].astype(F32))
    o_ref[...] = (ga * pa + gb * pb).astype(o_ref.dtype)


def _gate_proj(oa, ob, z, ga_off, gb_off, wa, wb):
    m, ka = oa.shape
    kb = ob.shape[1]
    n = wa.shape[1]
    tm = _tile(m, 1024)
    tn = _tile(n, 512)
    assert ga_off % tn == 0 and gb_off % tn == 0
    vm = (2 * (_nbytes((tm, ka + kb), BF16) + _nbytes((ka + kb, tn), BF16) + 3 * _nbytes((tm, tn), BF16))
          + 4 * _nbytes((tm, tn), F32))
    act = lambda k: pl.BlockSpec((tm, k), lambda i, j: (i, 0))
    wsp = lambda k: pl.BlockSpec((k, tn), lambda i, j: (0, j))
    return pl.pallas_call(
        _gate_proj_kernel,
        grid=(m // tm, n // tn),
        in_specs=[act(ka), act(kb),
                  pl.BlockSpec((tm, tn), lambda i, j: (i, ga_off // tn + j)),
                  pl.BlockSpec((tm, tn), lambda i, j: (i, gb_off // tn + j)),
                  wsp(ka), wsp(kb)],
        out_specs=pl.BlockSpec((tm, tn), lambda i, j: (i, j)),
        out_shape=jax.ShapeDtypeStruct((m, n), BF16),
        compiler_params=_cparams(("parallel", "parallel"), vm),
        name="gate_proj",
    )(oa, ob, z, z, wa, wb)


def _out_ln_kernel(m_ref, w_ref, h_ref, g_ref, b_ref, ho_ref, *, alpha):
    y = alpha * h_ref[...] + _dot(m_ref[...], w_ref[...])
    ho_ref[...] = _ln_rows(y, g_ref[...], b_ref[...])


def _out_ln(merged, w, h, g, b, alpha):
    m, d = h.shape
    tm = _tile(m, 512)
    row = pl.BlockSpec((tm, d), lambda i: (i, 0))
    vec = pl.BlockSpec((1, d), lambda i: (0, 0))
    vm = 2 * (_nbytes((tm, d), BF16) + _nbytes(w.shape, BF16) + 2 * _nbytes((tm, d), F32)) + 3 * _nbytes((tm, d), F32)
    return pl.pallas_call(
        functools.partial(_out_ln_kernel, alpha=alpha),
        grid=(m // tm,),
        in_specs=[row, pl.BlockSpec(w.shape, lambda i: (0, 0)), row, vec, vec],
        out_specs=row,
        out_shape=jax.ShapeDtypeStruct((m, d), F32),
        compiler_params=_cparams(("parallel",), vm),
        name="out_proj_ln",
    )(merged, w, h, g.reshape(1, d), b.reshape(1, d))


def _swiglu(a1, a3):
    return a1 * jax.nn.sigmoid(a1) * a3


def _ffn_ln_kernel(h_ref, w1_ref, w3_ref, w2_ref, lg_ref, lb_ref, ho_ref, hbo_ref, xb_ref, acc_ref, *, alpha):
    j = pl.program_id(1)

    @pl.when(j == 0)
    def _():
        xb_ref[...] = h_ref[...].astype(BF16)
        acc_ref[...] = jnp.zeros_like(acc_ref)

    xb = xb_ref[...]
    gact = _swiglu(_dot(xb, w1_ref[...]), _dot(xb, w3_ref[...])).astype(BF16)
    acc_ref[...] += _dot(gact, w2_ref[...])

    @pl.when(j == pl.num_programs(1) - 1)
    def _():
        y = _ln_rows(alpha * h_ref[...] + acc_ref[...], lg_ref[...], lb_ref[...])
        ho_ref[...] = y
        hbo_ref[...] = y.astype(BF16)


def _ffn_ln(h, w1, w3, w2, lg, lb, alpha):
    m, d = h.shape
    f = w1.shape[1]
    tm = _tile(m, 512)
    tf = _tile(f, 512)
    vm = (2 * (2 * _nbytes((tm, d), F32) + 3 * _nbytes((d, tf), BF16) + _nbytes((tm, d), BF16))
          + _nbytes((tm, d), BF16) + _nbytes((tm, d), F32) + 4 * _nbytes((tm, tf), F32) + 2 * _nbytes((tm, d), F32))
    row = pl.BlockSpec((tm, d), lambda i, j: (i, 0))
    vec = pl.BlockSpec((1, d), lambda i, j: (0, 0))
    wup = pl.BlockSpec((d, tf), lambda i, j: (0, j))
    return pl.pallas_call(
        functools.partial(_ffn_ln_kernel, alpha=alpha),
        grid=(m // tm, f // tf),
        in_specs=[row, wup, wup, pl.BlockSpec((tf, d), lambda i, j: (j, 0)), vec, vec],
        out_specs=[row, row],
        out_shape=[jax.ShapeDtypeStruct((m, d), F32), jax.ShapeDtypeStruct((m, d), BF16)],
        scratch_shapes=[pltpu.VMEM((tm, d), BF16), pltpu.VMEM((tm, d), F32)],
        compiler_params=_cparams(("parallel", "arbitrary"), vm),
        name="ffn_ln",
    )(h, w1, w3, w2, lg.reshape(1, d), lb.reshape(1, d))


_PK_E1, _PK_E2, _PK_G1, _PK_G2, _PK_R1, _PK_R2 = range(6)


def _router_kernel(h_ref, w_ref, b_ref, pk_ref, cnt_ref, carry_ref, *, n_exp):
    @pl.when(pl.program_id(0) == 0)
    def _():
        carry_ref[...] = jnp.zeros_like(carry_ref)

    logits = jnp.dot(h_ref[...], w_ref[...], preferred_element_type=F32, precision=lax.Precision.HIGHEST) + b_ref[...]
    tm, nl = logits.shape
    lane = lax.broadcasted_iota(jnp.int32, (tm, nl), 1).astype(F32)
    s = jnp.where(lane < n_exp, logits, -jnp.inf)
    m1 = jnp.max(s, axis=-1, keepdims=True)
    i1 = jnp.min(jnp.where(s == m1, lane, float(nl)), axis=-1, keepdims=True)
    s2 = jnp.where(lane == i1, -jnp.inf, s)
    m2 = jnp.max(s2, axis=-1, keepdims=True)
    i2 = jnp.min(jnp.where(s2 == m2, lane, float(nl)), axis=-1, keepdims=True)
    e = jnp.exp(m2 - m1)
    g1 = 1.0 / (1.0 + e)
    g2 = e / (1.0 + e)
    oh1 = lane == i1
    oh2 = lane == i2
    both = jnp.where(oh1 | oh2, 1.0, 0.0)
    r = lax.broadcasted_iota(jnp.int32, (tm, tm), 0)
    c = lax.broadcasted_iota(jnp.int32, (tm, tm), 1)
    tri = jnp.where(c < r, 1.0, 0.0).astype(BF16)
    prefix = _dot(tri, both.astype(BF16)) + carry_ref[...]
    r1 = jnp.sum(jnp.where(oh1, prefix, 0.0), axis=-1, keepdims=True)
    r2 = jnp.sum(jnp.where(oh2, prefix, 0.0), axis=-1, keepdims=True)
    carry_ref[...] += jnp.sum(both, axis=0, keepdims=True)
    cnt_ref[...] = carry_ref[...]
    pk = jnp.zeros((tm, nl), F32)
    for col, val in ((_PK_E1, i1), (_PK_E2, i2), (_PK_G1, g1), (_PK_G2, g2), (_PK_R1, r1), (_PK_R2, r2)):
        pk = jnp.where(lane == col, val, pk)
    pk_ref[...] = pk


def _router(h, w_pad, b_pad, n_exp):
    m, d = h.shape
    nl = w_pad.shape[1]
    tm = _tile(m, 512)
    vm = 2 * (_nbytes((tm, d), F32) + _nbytes((d, nl), F32) + _nbytes((tm, nl), F32)) + _nbytes((tm, tm), F32) * 3
    return pl.pallas_call(
        functools.partial(_router_kernel, n_exp=n_exp),
        grid=(m // tm,),
        in_specs=[pl.BlockSpec((tm, d), lambda i: (i, 0)), pl.BlockSpec((d, nl), lambda i: (0, 0)),
                  pl.BlockSpec((1, nl), lambda i: (0, 0))],
        out_specs=[pl.BlockSpec((tm, nl), lambda i: (i, 0)), pl.BlockSpec((1, nl), lambda i: (0, 0))],
        out_shape=[jax.ShapeDtypeStruct((m, nl), F32), jax.ShapeDtypeStruct((1, nl), F32)],
        scratch_shapes=[pltpu.VMEM((1, nl), F32)],
        compiler_params=_cparams(("arbitrary",), vm),
        name="moe_router",
    )(h, w_pad, b_pad)


def _row_copy(src_ref, src_row, dst_ref, dst_row, sem):
    return pltpu.make_async_copy(src_ref.at[pl.ds(src_row, 1)], dst_ref.at[pl.ds(dst_row, 1)], sem)


DMA_ISSUE_UNROLL = 8


def _dispatch_kernel(d1_ref, d2_ref, pad_ref, h_ref, zero_ref, xs_ref, sem, *, n_ranges):
    tb = h_ref.shape[0]

    def issue(t, carry):
        _row_copy(h_ref, t, xs_ref, d1_ref[0, 0, t], sem).start()
        _row_copy(h_ref, t, xs_ref, d2_ref[0, 0, t], sem).start()
        return carry

    lax.fori_loop(0, tb, issue, 0, unroll=DMA_ISSUE_UNROLL)

    @pl.when(pl.program_id(0) == 0)
    def _():
        for e in range(n_ranges):
            base = pad_ref[0, e]
            n = pad_ref[1, e]
            lax.fori_loop(0, n, lambda j, c: (_row_copy(zero_ref, 0, xs_ref, base + j, sem).start(), c)[1], 0)
            lax.fori_loop(0, n, lambda j, c: (_row_copy(zero_ref, 0, xs_ref, 0, sem).wait(), c)[1], 0)

    for _ in range(TOP_K):
        pltpu.make_async_copy(h_ref, xs_ref.at[pl.ds(0, tb)], sem).wait()


def _dispatch(h, d1, d2, pad_info, n_slots):
    m, d = h.shape
    tb = _tile(m, 512)
    nt = m // tb
    idx = pl.BlockSpec((1, 1, tb), lambda i: (i, 0, 0), memory_space=pltpu.SMEM)
    zero = jnp.zeros((8, d), F32)
    return pl.pallas_call(
        functools.partial(_dispatch_kernel, n_ranges=pad_info.shape[1]),
        grid=(nt,),
        in_specs=[idx, idx, pl.BlockSpec(memory_space=pltpu.SMEM), pl.BlockSpec((tb, d), lambda i: (i, 0)),
                  pl.BlockSpec((8, d), lambda i: (0, 0))],
        out_specs=pl.BlockSpec(memory_space=pl.ANY),
        out_shape=jax.ShapeDtypeStruct((n_slots, d), F32),
        scratch_shapes=[pltpu.SemaphoreType.DMA(())],
        compiler_params=_cparams(("arbitrary",), 4 * tb * d * 4),
        name="moe_dispatch",
    )(d1.reshape(nt, 1, tb), d2.reshape(nt, 1, tb), pad_info, h, zero)


def _moe_ffn_kernel(te_ref, ts_ref, tv_ref, x_ref, w1_ref, w3_ref, w2_ref, y_ref, xb_ref):
    i = pl.program_id(0)
    j = pl.program_id(1)

    @pl.when(tv_ref[i] != 0)
    def _():
        @pl.when(j == 0)
        def _():
            xb_ref[...] = x_ref[...].astype(BF16)
            y_ref[...] = jnp.zeros_like(y_ref)

        xb = xb_ref[...]
        gact = _swiglu(_dot(xb, w1_ref[0]), _dot(xb, w3_ref[0])).astype(BF16)
        y_ref[...] += _dot(gact, w2_ref[0])

    @pl.when((tv_ref[i] == 0) & (j == 0))
    def _():
        y_ref[...] = jnp.zeros_like(y_ref)


def _moe_ffn(xs, w1, w3, w2, tile_expert, tile_src, tile_valid, tm):
    n_slots, d = xs.shape
    f = w1.shape[2]
    tf = _tile(f, 512)
    nf = f // tf
    n_tiles = n_slots // tm

    def jj(i, j, tv):
        return jnp.where(tv[i] != 0, j, nf - 1)

    vm = (2 * (2 * _nbytes((tm, d), F32) + 3 * _nbytes((d, tf), BF16)) + _nbytes((tm, d), BF16)
          + 4 * _nbytes((tm, tf), F32) + _nbytes((tm, d), F32))
    grid_spec = pltpu.PrefetchScalarGridSpec(
        num_scalar_prefetch=3,
        grid=(n_tiles, nf),
        in_specs=[pl.BlockSpec((tm, d), lambda i, j, te, ts, tv: (ts[i], 0)),
                  pl.BlockSpec((1, d, tf), lambda i, j, te, ts, tv: (te[i], 0, jj(i, j, tv))),
                  pl.BlockSpec((1, d, tf), lambda i, j, te, ts, tv: (te[i], 0, jj(i, j, tv))),
                  pl.BlockSpec((1, tf, d), lambda i, j, te, ts, tv: (te[i], jj(i, j, tv), 0))],
        out_specs=pl.BlockSpec((tm, d), lambda i, j, te, ts, tv: (i, 0)),
        scratch_shapes=[pltpu.VMEM((tm, d), BF16)],
    )
    return pl.pallas_call(
        _moe_ffn_kernel,
        grid_spec=grid_spec,
        out_shape=jax.ShapeDtypeStruct((n_slots, d), F32),
        compiler_params=_cparams(("arbitrary", "arbitrary"), vm),
        name="moe_expert_ffn",
    )(tile_expert, tile_src, tile_valid, xs, w1, w3, w2)


def _combine_ln_kernel(d1_ref, d2_ref, d1n_ref, d2n_ref, pk_ref, h_ref, y_ref, lg_ref, lb_ref, o_ref, ybuf_ref, sems,
                       *, alpha, tb):
    step = pl.program_id(0)

    def gather(i1_ref, i2_ref, off, buf):
        def issue(t, carry):
            _row_copy(y_ref, i1_ref[0, 0, off + t], ybuf_ref.at[buf, 0], t, sems.at[buf]).start()
            _row_copy(y_ref, i2_ref[0, 0, off + t], ybuf_ref.at[buf, 1], t, sems.at[buf]).start()
            return carry

        lax.fori_loop(0, tb, issue, 0, unroll=DMA_ISSUE_UNROLL)

    def wait(buf):
        for k in range(TOP_K):
            pltpu.make_async_copy(y_ref.at[pl.ds(0, tb)], ybuf_ref.at[buf, k], sems.at[buf]).wait()

    def finish(buf, rows):
        pk = pk_ref[rows, :]
        f = pk[:, _PK_G1:_PK_G1 + 1] * ybuf_ref[buf, 0] + pk[:, _PK_G2:_PK_G2 + 1] * ybuf_ref[buf, 1]
        o_ref[rows, :] = _ln_rows(alpha * h_ref[rows, :] + f, lg_ref[...], lb_ref[...])

    @pl.when(step == 0)
    def _():
        gather(d1_ref, d2_ref, 0, 0)

    gather(d1_ref, d2_ref, tb, 1)
    wait(0)
    finish(0, slice(0, tb))

    @pl.when(step < pl.num_programs(0) - 1)
    def _():
        gather(d1n_ref, d2n_ref, 0, 0)

    wait(1)
    finish(1, slice(tb, 2 * tb))


def _combine_ln(h, pk, ys, d1, d2, lg, lb, alpha):
    m, d = h.shape
    tb = _tile(m // 2, 256)
    ns = m // (2 * tb)
    idx = pl.BlockSpec((1, 1, 2 * tb), lambda i: (i, 0, 0), memory_space=pltpu.SMEM)
    idx_next = pl.BlockSpec((1, 1, 2 * tb), lambda i: (jnp.minimum(i + 1, ns - 1), 0, 0), memory_space=pltpu.SMEM)
    row = pl.BlockSpec((2 * tb, d), lambda i: (i, 0))
    vec = pl.BlockSpec((1, d), lambda i: (0, 0))
    vm = (2 * (2 * _nbytes((2 * tb, d), F32) + _nbytes((2 * tb, pk.shape[1]), F32)) + 4 * _nbytes((tb, d), F32)
          + 4 * _nbytes((tb, d), F32))
    d1b, d2b = d1.reshape(ns, 1, 2 * tb), d2.reshape(ns, 1, 2 * tb)
    return pl.pallas_call(
        functools.partial(_combine_ln_kernel, alpha=alpha, tb=tb),
        grid=(ns,),
        in_specs=[idx, idx, idx_next, idx_next, pl.BlockSpec((2 * tb, pk.shape[1]), lambda i: (i, 0)), row,
                  pl.BlockSpec(memory_space=pl.ANY), vec, vec],
        out_specs=row,
        out_shape=jax.ShapeDtypeStruct((m, d), F32),
        scratch_shapes=[pltpu.VMEM((2, TOP_K, tb, d), F32), pltpu.SemaphoreType.DMA((2,))],
        compiler_params=_cparams(("arbitrary",), vm),
        name="moe_combine_ln",
    )(d1b, d2b, d1b, d2b, pk, h, ys, lg.reshape(1, d), lb.reshape(1, d))


def _rope_tables(n_tok):
    pos = jnp.arange(n_tok, dtype=F32)
    inv = ROPE_THETA ** (-jnp.arange(0, MLA_ROPE_DIM, 2, dtype=F32) / MLA_ROPE_DIM)
    ang = pos[:, None] * inv[None, :]
    cos, sin = jnp.cos(ang), jnp.sin(ang)
    zero = jnp.zeros_like(cos)
    return jnp.concatenate([cos, cos, zero, zero], -1), jnp.concatenate([-sin, sin, zero, zero], -1)


def _rel_bucket(rel):
    nb = N_BUCKETS // 2
    max_exact = nb // 2
    n = jnp.abs(rel)
    large = max_exact + (jnp.log(jnp.maximum(n, 1).astype(F32) / max_exact)
                         / math.log(MAX_DISTANCE / max_exact) * (nb - max_exact)).astype(jnp.int32)
    large = jnp.minimum(large, nb - 1)
    return jnp.where(rel > 0, nb, 0) + jnp.where(n < max_exact, n, large)


def _swa_index_tables(n_meta, n_real):
    nblk = n_real // BLOCK
    band = jnp.arange(3 * BLOCK)
    meta_pos = jnp.arange(n_meta)
    pad = BLOCK - n_meta

    def one(start, q_pos, first_block_only=False):
        r_key = start - BLOCK + band
        in_range = (r_key >= 0) & (r_key < n_real)
        if first_block_only:
            in_range = in_range & (r_key < BLOCK)
        k_pos = jnp.concatenate([n_meta + r_key, meta_pos, jnp.zeros((pad,), jnp.int32)])
        always = jnp.concatenate([jnp.zeros((3 * BLOCK,), bool), jnp.ones((n_meta,), bool), jnp.zeros((pad,), bool)])
        live = jnp.concatenate([in_range, jnp.ones((n_meta,), bool), jnp.zeros((pad,), bool)])
        rel = k_pos[None, :] - q_pos[:, None]
        vis = always[None, :] | (live[None, :] & (jnp.abs(rel) <= WINDOW))
        return _rel_bucket(rel), vis

    q_local = jnp.arange(BLOCK)
    tabs = [one(0, n_meta + q_local),
            one(BLOCK * min(1, nblk - 1), n_meta + BLOCK * min(1, nblk - 1) + q_local),
            one(BLOCK * (nblk - 1), n_meta + BLOCK * (nblk - 1) + q_local),
            one(0, jnp.where(q_local < n_meta, q_local, 0), first_block_only=True)]
    bucket = jnp.stack([t[0] for t in tabs]).astype(jnp.int32)
    vis = jnp.stack([t[1] for t in tabs]).astype(jnp.int32)
    return bucket, vis


def _layer_weights(l, w_in, w_uq, w_ukv, w_proj_a, w_proj_b, w_out, d_model, q_rank, kv_rank):
    hd = SWA_HEADS * SWA_HEAD_DIM
    kvd = SWA_KV_HEADS * SWA_HEAD_DIM
    splits = (q_rank, kv_rank, MLA_ROPE_DIM, hd, kvd, kvd, d_model, d_model)
    off = [0]
    for s in splits:
        off.append(off[-1] + s)
    col = lambda i: w_in[l][:, off[i]:off[i + 1]]
    order = (3, 6, 7, 1, 4, 5, 0)
    w_z = jnp.concatenate([col(i) for i in order], axis=1).astype(BF16)
    z_off = {}
    o = 0
    for i in order:
        z_off[i] = o
        o += splits[i]
    half = MLA_ROPE_DIM // 2
    kr = col(2)
    w_kr = jnp.concatenate([kr[:, :half], kr[:, half:], kr[:, :half], kr[:, half:]], axis=1).astype(BF16)
    uq = w_uq[l].reshape(q_rank, MLA_HEADS, MLA_NOPE_DIM + MLA_ROPE_DIM)
    uq_n = uq[:, :, :MLA_NOPE_DIM].reshape(q_rank, MLA_HEADS * MLA_NOPE_DIM).astype(BF16)
    r1 = uq[:, :, MLA_NOPE_DIM:MLA_NOPE_DIM + half]
    r2 = uq[:, :, MLA_NOPE_DIM + half:]
    uq_r = jnp.concatenate([r1, r2, r1, r2], axis=2).reshape(q_rank, MLA_HEADS * V7X_LANES).astype(BF16)
    ukv = w_ukv[l].reshape(kv_rank, MLA_HEADS, MLA_NOPE_DIM + MLA_V_DIM)
    uk = ukv[:, :, :MLA_NOPE_DIM].reshape(kv_rank, MLA_HEADS * MLA_NOPE_DIM).astype(BF16)
    uv = ukv[:, :, MLA_NOPE_DIM:].reshape(kv_rank, MLA_HEADS * MLA_V_DIM).astype(BF16)
    return dict(w_z=w_z, z_off=z_off, w_kr=w_kr, uq_n=uq_n, uq_r=uq_r, uk=uk, uv=uv, uv_t=uv.T,
                wa=w_proj_a[l].astype(BF16), wb=w_proj_b[l].astype(BF16), wo=w_out[l].astype(BF16))


def kernel(x, meta_tokens, emb_ln_g, emb_ln_b, rel_bias, w_in, q_norm_g, kv_norm_g, w_uq, w_ukv, sink_logits,
           w_proj_a, w_proj_b, w_out, ln_mix_g, ln_mix_b, ln_ffn_g, ln_ffn_b, ffn_w1, ffn_w3, ffn_w2, router_w,
           router_b, moe_w1, moe_w3, moe_w2):
    bsz, seq, d = x.shape
    n_meta = meta_tokens.shape[0]
    depth = w_in.shape[0]
    q_rank = q_norm_g.shape[1]
    kv_rank = kv_norm_g.shape[1]
    alpha = (2 * depth) ** 0.25
    nblk = seq // BLOCK
    assert seq % BLOCK == 0 and n_meta % 16 == 0 and n_meta <= BLOCK

    hr, hr_b = _embed_ln(x.reshape(bsz * seq, d), emb_ln_g, emb_ln_b)
    hm1, hm1_b = _embed_ln(meta_tokens.astype(x.dtype), emb_ln_g, emb_ln_b)
    hm, hm_b = jnp.tile(hm1, (bsz, 1)), jnp.tile(hm1_b, (bsz, 1))

    ct, st = _rope_tables(n_meta + seq)
    ct_r, st_r = ct[n_meta:], st[n_meta:]
    ct_m, st_m = jnp.tile(ct[:n_meta], (bsz, 1)), jnp.tile(st[:n_meta], (bsz, 1))
    bucket, vis = _swa_index_tables(n_meta, seq)
    bias_all = _bias_tables(rel_bias, bucket, vis)
    bias_real = bias_all[:3]
    bias_meta = bias_all[3:, :, :n_meta]

    last_blk = nblk - 1
    type_real = lambda i: jnp.where(i == 0, 0, jnp.where(i == last_blk, 2, 1))
    kv_real = lambda i: (jnp.maximum(i - 1, 0), i, jnp.minimum(i + 1, last_blk))
    type_meta = lambda i: 0
    kv_meta_q = lambda i: (0, 0, 0)

    for l in range(depth):
        last = l == depth - 1
        lw = _layer_weights(l, w_in, w_uq, w_ukv, w_proj_a, w_proj_b, w_out, d, q_rank, kv_rank)
        zo = lw["z_off"]
        sink = sink_logits[l]

        z_r = _matmul(hr_b, lw["w_z"], BF16)
        z_m = _matmul(hm_b, lw["w_z"], BF16)
        k_r, v_r, ksq_r = _kv_proj(hr_b, z_r, zo[1], kv_norm_g[l], lw["uk"], lw["uv_t"], lw["w_kr"], ct_r, st_r, True)
        k_m, v_m, ksq_m = _kv_proj(hm_b, z_m, zo[1], kv_norm_g[l], lw["uk"], lw["uv"], lw["w_kr"], ct_m, st_m, False)
        kmax = jnp.maximum(ksq_r.reshape(bsz, seq, -1).max(axis=1), ksq_m.reshape(bsz, n_meta, -1).max(axis=1))
        kmax_r, kmax_m = jnp.repeat(kmax, seq, axis=0), jnp.repeat(kmax, n_meta, axis=0)
        q_r = _q_proj(z_r, zo[0], q_norm_g[l], lw["uq_n"], lw["uq_r"], ct_r, st_r, kmax_r)
        oa_r = _mla_attn(q_r, k_r, k_m, v_r, v_m, bsz, n_meta)
        ob_r = _swa_attn(sink, z_r, BLOCK, z_r, z_m, zo[4], zo[5], bias_real, type_real, kv_real, bsz, nblk, n_meta)
        mg_r = _gate_proj(oa_r, ob_r, z_r, zo[6], zo[7], lw["wa"], lw["wb"])
        hr = _out_ln(mg_r, lw["wo"], hr, ln_mix_g[l], ln_mix_b[l], alpha)
        if not last:
            q_m = _q_proj(z_m, zo[0], q_norm_g[l], lw["uq_n"], lw["uq_r"], ct_m, st_m, kmax_m)
            oa_m = _mla_attn(q_m, k_r, k_m, v_r, v_m, bsz, n_meta)
            ob_m = _swa_attn(sink, z_m, n_meta, z_r, z_m, zo[4], zo[5], bias_meta, type_meta, kv_meta_q, bsz, 1,
                             n_meta)
            mg_m = _gate_proj(oa_m, ob_m, z_m, zo[6], zo[7], lw["wa"], lw["wb"])
            hm = _out_ln(mg_m, lw["wo"], hm, ln_mix_g[l], ln_mix_b[l], alpha)

        if l % 2 == 0:
            w1, w3, w2 = (ffn_w1[l // 2].astype(BF16), ffn_w3[l // 2].astype(BF16), ffn_w2[l // 2].astype(BF16))
            hr, hr_b = _ffn_ln(hr, w1, w3, w2, ln_ffn_g[l], ln_ffn_b[l], alpha)
            if not last:
                hm, hm_b = _ffn_ln(hm, w1, w3, w2, ln_ffn_g[l], ln_ffn_b[l], alpha)
        else:
            hr = _moe_layer(hr, router_w[l // 2], router_b[l // 2], moe_w1[l // 2], moe_w3[l // 2], moe_w2[l // 2],
                            ln_ffn_g[l], ln_ffn_b[l], alpha)
            hr_b = hr.astype(BF16)
            if not last:
                hm = _moe_layer(hm, router_w[l // 2], router_b[l // 2], moe_w1[l // 2], moe_w3[l // 2],
                                moe_w2[l // 2], ln_ffn_g[l], ln_ffn_b[l], alpha)
                hm_b = hm.astype(BF16)
    return hr.reshape(bsz, seq, d)


def _moe_layer(h, router_w, router_b, w1, w3, w2, lg, lb, alpha):
    m, d = h.shape
    n_exp = router_w.shape[1]
    tm = _tile(m, 512)
    w_pad = jnp.zeros((d, V7X_LANES), F32).at[:, :n_exp].set(router_w)
    b_pad = jnp.zeros((1, V7X_LANES), F32).at[0, :n_exp].set(router_b)
    pk, cnt = _router(h, w_pad, b_pad, n_exp)
    counts = cnt[0, :n_exp].astype(jnp.int32)
    padded = (counts + tm - 1) // tm * tm
    pad_end = jnp.cumsum(padded)
    pad_start = pad_end - padded
    e1 = pk[:, _PK_E1].astype(jnp.int32)
    e2 = pk[:, _PK_E2].astype(jnp.int32)
    d1 = pad_start[e1] + pk[:, _PK_R1].astype(jnp.int32)
    d2 = pad_start[e2] + pk[:, _PK_R2].astype(jnp.int32)
    n_tiles = (m * TOP_K) // tm + n_exp
    n_slots = n_tiles * tm
    n_used = pad_end[-1] // tm
    tile_id = jnp.arange(n_tiles, dtype=jnp.int32)
    tile_valid = (tile_id < n_used).astype(jnp.int32)
    tile_src = jnp.minimum(tile_id, n_used - 1).astype(jnp.int32)
    tile_expert = jnp.minimum(jnp.searchsorted(pad_end, tile_src * tm, side="right"), n_exp - 1).astype(jnp.int32)
    pad_info = jnp.stack([jnp.append(pad_start + counts, pad_end[-1]),
                          jnp.append(padded - counts, n_slots - pad_end[-1])]).astype(jnp.int32)

    xs = _dispatch(h, d1, d2, pad_info, n_slots)
    ys = _moe_ffn(xs, w1.astype(BF16), w3.astype(BF16), w2.astype(BF16), tile_expert, tile_src, tile_valid, tm)
    return _combine_ln(h, pk, ys, d1, d2, lg, lb, alpha)
```

```python
import functools
import math

import jax
import jax.numpy as jnp
from jax import lax
from jax.experimental import pallas as pl
from jax.experimental.pallas import tpu as pltpu

MLA_HEADS = 16
MLA_NOPE_DIM = 128
MLA_ROPE_DIM = 64
MLA_V_DIM = 128
SWA_HEADS = 16
SWA_KV_HEADS = 4
SWA_HEAD_DIM = 128
WINDOW = 128
BLOCK = 128
N_BUCKETS = 32
MAX_DISTANCE = 128
TOP_K = 2
ROPE_THETA = 10000.0
LN_EPS = 1e-5
RMS_EPS = 1e-6
NEG = -1e30
LOG2E = math.log2(math.e)

V7X_LANES = 128
V7X_MXU_DIM = 256
V7X_VMEM_LIMIT_BYTES = 56 * 1024 * 1024

F32 = jnp.float32
BF16 = jnp.bfloat16


COMPILER_SCRATCH_BYTES = 8 * 1024 * 1024


def _cparams(semantics, vmem_bytes):
    limit = int(min(V7X_VMEM_LIMIT_BYTES, vmem_bytes + COMPILER_SCRATCH_BYTES))
    return pltpu.CompilerParams(dimension_semantics=semantics, vmem_limit_bytes=limit)


def _nbytes(shape, dtype):
    return math.prod(shape) * jnp.dtype(dtype).itemsize


def _dot(a, b):
    return jnp.dot(a, b, preferred_element_type=F32)


def _dot_t(a, b):
    return lax.dot_general(a, b, (((1,), (1,)), ((), ())), preferred_element_type=F32)


def _tile(n, pref):
    t = min(n, pref)
    while n % t:
        t //= 2
    return t


def _ln_rows(y, g, b):
    mu = jnp.mean(y, axis=-1, keepdims=True)
    d = y - mu
    var = jnp.mean(d * d, axis=-1, keepdims=True)
    return d * lax.rsqrt(var + LN_EPS) * g + b


def _embed_ln_kernel(x_ref, g_ref, b_ref, h_ref, hb_ref):
    y = _ln_rows(x_ref[...], g_ref[...], b_ref[...])
    h_ref[...] = y
    hb_ref[...] = y.astype(BF16)


def _embed_ln(x, g, b):
    m, d = x.shape
    tm = _tile(m, 512)
    row = pl.BlockSpec((tm, d), lambda i: (i, 0))
    vec = pl.BlockSpec((1, d), lambda i: (0, 0))
    return pl.pallas_call(
        _embed_ln_kernel,
        grid=(m // tm,),
        in_specs=[row, vec, vec],
        out_specs=[row, row],
        out_shape=[jax.ShapeDtypeStruct((m, d), F32), jax.ShapeDtypeStruct((m, d), BF16)],
        compiler_params=_cparams(("parallel",), 2 * tm * d * 10 + 4 * tm * d * 4),
        name="embed_ln",
    )(x, g.reshape(1, d), b.reshape(1, d))


def _mm_kernel(a_ref, w_ref, o_ref):
    o_ref[...] = _dot(a_ref[...], w_ref[...]).astype(o_ref.dtype)


def _matmul(a, w, out_dtype, tm_pref=2048, tn_pref=768):
    m, k = a.shape
    n = w.shape[1]
    tm = _tile(m, tm_pref)
    tn = next(t for t in (tn_pref, 512, 256, 128) if n % t == 0)
    vm = 2 * (_nbytes((tm, k), BF16) + _nbytes((k, tn), BF16) + _nbytes((tm, tn), out_dtype)) + _nbytes((tm, tn), F32)
    return pl.pallas_call(
        _mm_kernel,
        grid=(m // tm, n // tn),
        in_specs=[pl.BlockSpec((tm, k), lambda i, j: (i, 0)), pl.BlockSpec((k, tn), lambda i, j: (0, j))],
        out_specs=pl.BlockSpec((tm, tn), lambda i, j: (i, j)),
        out_shape=jax.ShapeDtypeStruct((m, n), out_dtype),
        compiler_params=_cparams(("parallel", "parallel"), vm),
        name="in_proj",
    )(a, w)


def _rope(x, ct, st):
    return x * ct + pltpu.roll(x, MLA_ROPE_DIM // 2, 1) * st


def _rms(c, g):
    return c * lax.rsqrt(jnp.mean(c * c, axis=-1, keepdims=True) + RMS_EPS) * g


SHIFT_LANE = MLA_ROPE_DIM
BOUND_MARGIN = 1.0 + 2.0 ** -7


def _sq_norm(*parts):
    sq = sum(jnp.square(p.astype(F32)) for p in parts)
    return jnp.sum(sq, axis=-1, keepdims=True)


def _q_proj_kernel(c_ref, g_ref, wn_ref, wr_ref, ct_ref, st_ref, km_ref, hsum_ref, hput_ref, o_ref, *, heads, scale):
    cb = _rms(c_ref[...].astype(F32), g_ref[...]).astype(BF16)
    ct = ct_ref[...]
    st = st_ref[...]
    qn = (_dot(cb, wn_ref[...]) * scale).astype(BF16)
    qr_f = _dot(cb, wr_ref[...])
    qr = jnp.concatenate([(_rope(qr_f[:, h * V7X_LANES:(h + 1) * V7X_LANES], ct, st) * scale).astype(BF16)
                          for h in range(heads)], axis=1)
    sq = (jnp.square(qn.astype(F32)) + jnp.square(qr.astype(F32))).astype(BF16)
    bound = jnp.sqrt(_dot(sq, hsum_ref[...]) * km_ref[...]) * BOUND_MARGIN
    shift = _dot((-bound).astype(BF16), hput_ref[...])
    qr = (qr.astype(F32) + shift).astype(BF16)
    w = MLA_NOPE_DIM + V7X_LANES
    for h in range(heads):
        o_ref[:, h * w:h * w + MLA_NOPE_DIM] = qn[:, h * MLA_NOPE_DIM:(h + 1) * MLA_NOPE_DIM]
        o_ref[:, h * w + MLA_NOPE_DIM:(h + 1) * w] = qr[:, h * V7X_LANES:(h + 1) * V7X_LANES]


def _q_proj(z, cq_off, g, wn, wr, ct, st, kmax_rows):
    m = z.shape[0]
    r = g.shape[0]
    heads = MLA_HEADS
    tm = _tile(math.gcd(m, ct.shape[0]), 512)
    npos = ct.shape[0] // tm
    assert cq_off % r == 0
    wout = heads * (MLA_NOPE_DIM + V7X_LANES)
    scale = (MLA_NOPE_DIM + MLA_ROPE_DIM) ** -0.5 * LOG2E
    vm = (2 * (_nbytes((tm, r), BF16) + _nbytes(wn.shape, BF16) + _nbytes(wr.shape, BF16) + 2 * tm * V7X_LANES * 4
               + _nbytes((tm, wout), BF16)) + 2 * _nbytes((tm, wn.shape[1]), F32) + _nbytes((tm, r), F32) * 2)
    tab = pl.BlockSpec((tm, V7X_LANES), lambda i: (i % npos, 0))
    assert MLA_NOPE_DIM == V7X_LANES and heads <= V7X_LANES
    chan = jnp.arange(heads * V7X_LANES)
    head_id = jnp.arange(V7X_LANES)
    hsum = (chan[:, None] // V7X_LANES == head_id[None, :]).astype(BF16)
    hput = (head_id[:, None] * V7X_LANES + SHIFT_LANE == chan[None, :]).astype(BF16)
    const = lambda a: pl.BlockSpec(a.shape, lambda i: (0, 0))
    return pl.pallas_call(
        functools.partial(_q_proj_kernel, heads=heads, scale=scale),
        grid=(m // tm,),
        in_specs=[pl.BlockSpec((tm, r), lambda i: (i, cq_off // r)),
                  pl.BlockSpec((1, r), lambda i: (0, 0)),
                  const(wn), const(wr),
                  tab, tab, pl.BlockSpec((tm, V7X_LANES), lambda i: (i, 0)), const(hsum), const(hput)],
        out_specs=pl.BlockSpec((tm, wout), lambda i: (i, 0)),
        out_shape=jax.ShapeDtypeStruct((m, wout), BF16),
        compiler_params=_cparams(("parallel",), vm),
        name="mla_q_proj",
    )(z, g.reshape(1, r), wn, wr, ct, st, kmax_rows, hsum, hput)


def _kv_proj_kernel(hb_ref, c_ref, g_ref, wk_ref, wv_ref, wkr_ref, ct_ref, st_ref, k_ref, v_ref, ksq_ref, *, heads,
                    transpose_v):
    cb = _rms(c_ref[...].astype(F32), g_ref[...]).astype(BF16)
    kn = _dot(cb, wk_ref[...])
    if transpose_v:
        v_ref[...] = _dot_t(wv_ref[...], cb).astype(BF16)
    else:
        v_ref[...] = _dot(cb, wv_ref[...]).astype(BF16)
    kr = _rope(_dot(hb_ref[...], wkr_ref[...]), ct_ref[...], st_ref[...]).astype(BF16)
    lane = lax.broadcasted_iota(jnp.int32, kr.shape, 1)
    kr_one = jnp.where(lane == SHIFT_LANE, 1.0, kr).astype(BF16)
    kr_sq = _sq_norm(kr)
    ksq = jnp.zeros(kr.shape, F32)
    w = MLA_NOPE_DIM + V7X_LANES
    for h in range(heads):
        kn_h = kn[:, h * MLA_NOPE_DIM:(h + 1) * MLA_NOPE_DIM].astype(BF16)
        k_ref[:, h * w:h * w + MLA_NOPE_DIM] = kn_h
        k_ref[:, h * w + MLA_NOPE_DIM:(h + 1) * w] = kr_one
        ksq = jnp.where(lane == h, _sq_norm(kn_h) + kr_sq, ksq)
    ksq_ref[...] = ksq


def _kv_proj(hb, z, ckv_off, g, wk, wv, wkr, ct, st, transpose_v):
    m, d = hb.shape
    r = g.shape[0]
    heads = MLA_HEADS
    tm = _tile(math.gcd(m, ct.shape[0]), 512)
    npos = ct.shape[0] // tm
    assert ckv_off % r == 0
    wk_out = heads * (MLA_NOPE_DIM + V7X_LANES)
    wv_out = heads * MLA_V_DIM
    if transpose_v:
        v_spec = pl.BlockSpec((wv_out, tm), lambda i: (0, i))
        v_shape = jax.ShapeDtypeStruct((wv_out, m), BF16)
    else:
        v_spec = pl.BlockSpec((tm, wv_out), lambda i: (i, 0))
        v_shape = jax.ShapeDtypeStruct((m, wv_out), BF16)
    vm = (2 * (_nbytes((tm, d), BF16) + _nbytes((tm, r), BF16) + _nbytes(wk.shape, BF16) + _nbytes(wv.shape, BF16)
               + _nbytes(wkr.shape, BF16) + 2 * tm * V7X_LANES * 4 + _nbytes((tm, wk_out), BF16)
               + _nbytes((tm, wv_out), BF16)) + 2 * _nbytes((tm, wv_out), F32) + _nbytes((tm, r), F32) * 2)
    tab = pl.BlockSpec((tm, V7X_LANES), lambda i: (i % npos, 0))
    return pl.pallas_call(
        functools.partial(_kv_proj_kernel, heads=heads, transpose_v=transpose_v),
        grid=(m // tm,),
        in_specs=[pl.BlockSpec((tm, d), lambda i: (i, 0)),
                  pl.BlockSpec((tm, r), lambda i: (i, ckv_off // r)),
                  pl.BlockSpec((1, r), lambda i: (0, 0)),
                  pl.BlockSpec(wk.shape, lambda i: (0, 0)),
                  pl.BlockSpec(wv.shape, lambda i: (0, 0)),
                  pl.BlockSpec(wkr.shape, lambda i: (0, 0)),
                  tab, tab],
        out_specs=[pl.BlockSpec((tm, wk_out), lambda i: (i, 0)), v_spec,
                   pl.BlockSpec((tm, V7X_LANES), lambda i: (i, 0))],
        out_shape=[jax.ShapeDtypeStruct((m, wk_out), BF16), v_shape, jax.ShapeDtypeStruct((m, V7X_LANES), F32)],
        compiler_params=_cparams(("parallel",), vm),
        name="mla_kv_proj",
    )(hb, z, g.reshape(1, r), wk, wv, wkr, ct, st)


def _with_ones(v):
    return jnp.concatenate([v, jnp.ones(v.shape, v.dtype)], axis=1)


ONES_ROWS = 16


UNDERFLOW_GUARD = 2.0 ** -100


def _mla_attn_kernel(q_ref, k_ref, km_ref, vt_ref, vm_ref, o_ref, m_ref, acc_ref, *, tk, n_meta, hb):
    wq = MLA_NOPE_DIM + V7X_LANES
    dv = MLA_V_DIM
    tq = acc_ref.shape[2]
    n_chunks = k_ref.shape[0] // tk
    pad = V7X_LANES - n_meta

    def q_of(j, shifted):
        q = q_ref[:, j * wq:(j + 1) * wq]
        if not shifted:
            lane = lax.broadcasted_iota(jnp.int32, q.shape, 1)
            q = jnp.where(lane == MLA_NOPE_DIM + SHIFT_LANE, 0.0, q).astype(BF16)
        if q.shape[0] < tq:
            q = jnp.concatenate([q, jnp.zeros((tq - q.shape[0], wq), BF16)], axis=0)
        return q

    def scores(j, c, shifted):
        start = pl.multiple_of(c * tk, tk)
        return _dot_t(k_ref[pl.ds(start, tk), j * wq:(j + 1) * wq], q_of(j, shifted))

    def vt_of(j, c):
        start = pl.multiple_of(c * tk, tk)
        return jnp.concatenate([vt_ref[j * dv:(j + 1) * dv, pl.ds(start, tk)], jnp.ones((ONES_ROWS, tk), BF16)], axis=0)

    def meta_scores(j, shifted):
        kmp = jnp.concatenate([km_ref[:, j * wq:(j + 1) * wq], jnp.zeros((pad, wq), BF16)], axis=0)
        s = _dot_t(kmp, q_of(j, shifted))
        row = lax.broadcasted_iota(jnp.int32, s.shape, 0)
        return jnp.where(row < n_meta, s, NEG)

    def meta_vt(j):
        vmp = jnp.concatenate([vm_ref[:, j * dv:(j + 1) * dv].astype(F32), jnp.zeros((pad, dv), F32)], axis=0)
        return jnp.concatenate([vmp.T.astype(BF16), jnp.ones((ONES_ROWS, V7X_LANES), BF16)], axis=0)

    def store(j, acc):
        o = (acc[:dv] / acc[dv:dv + 1]).T
        o_ref[:, j * dv:(j + 1) * dv] = o[:o_ref.shape[0]].astype(o_ref.dtype)

    acc = [_dot(meta_vt(j), jnp.exp2(meta_scores(j, True)).astype(BF16)) for j in range(hb)]
    s = [scores(j, 0, True) for j in range(hb)]
    for c in range(n_chunks):
        for j in range(hb):
            p = jnp.exp2(s[j]).astype(BF16)
            if c + 1 < n_chunks:
                s[j] = scores(j, c + 1, True)
            acc[j] = acc[j] + _dot(vt_of(j, c), p)
    for j in range(hb):
        store(j, acc[j])

    def online_head(j):
        def accumulate(c, sc):
            m = m_ref[j, 0:1]
            m_new = jnp.maximum(m, jnp.max(sc, axis=0, keepdims=True))
            acc_ref[j] = jnp.exp2(m - m_new) * acc_ref[j] + _dot(vt_of(j, c), jnp.exp2(sc - m_new).astype(BF16))
            m_ref[j, 0:1] = m_new

        sm = meta_scores(j, False)
        m0 = jnp.max(sm, axis=0, keepdims=True)
        m_ref[j, 0:1] = m0
        acc_ref[j] = _dot(meta_vt(j), jnp.exp2(sm - m0).astype(BF16))

        def body(c, carry):
            accumulate(c, scores(j, c, False))
            return carry

        lax.fori_loop(0, n_chunks, body, 0)
        store(j, acc_ref[j])

    for j in range(hb):
        denom_ok = jnp.min(acc[j][dv:dv + 1]) >= UNDERFLOW_GUARD

        @pl.when(jnp.logical_not(denom_ok))
        def _():
            online_head(j)


def _mla_attn(q, k, km, vt, vmeta, batch, n_meta):
    heads = MLA_HEADS
    hb = 2
    wq = MLA_NOPE_DIM + V7X_LANES
    dv = MLA_V_DIM
    rows_q = q.shape[0] // batch
    seq = k.shape[0] // batch
    tq = _tile(rows_q, 1024)
    tk = _tile(seq, 4 * V7X_MXU_DIM)
    nq = rows_q // tq
    tqp = max(tq, V7X_LANES)
    vm = (2 * hb * (_nbytes((tq, wq), BF16) + _nbytes((seq, wq), BF16) + _nbytes((seq, dv), BF16)
                    + _nbytes((tq, dv), BF16)) + hb * (6 * _nbytes((tk, tqp), F32) + 2 * _nbytes((2 * dv, tqp), F32)))
    return pl.pallas_call(
        functools.partial(_mla_attn_kernel, tk=tk, n_meta=n_meta, hb=hb),
        grid=(batch, heads // hb, nq),
        in_specs=[pl.BlockSpec((tq, hb * wq), lambda b, h, i: (b * nq + i, h)),
                  pl.BlockSpec((seq, hb * wq), lambda b, h, i: (b, h)),
                  pl.BlockSpec((n_meta, hb * wq), lambda b, h, i: (b, h)),
                  pl.BlockSpec((hb * dv, seq), lambda b, h, i: (h, b)),
                  pl.BlockSpec((n_meta, hb * dv), lambda b, h, i: (b, h))],
        out_specs=pl.BlockSpec((tq, hb * dv), lambda b, h, i: (b * nq + i, h)),
        out_shape=jax.ShapeDtypeStruct((q.shape[0], heads * dv), BF16),
        scratch_shapes=[pltpu.VMEM((hb, 8, tqp), F32), pltpu.VMEM((hb, dv + ONES_ROWS, tqp), F32)],
        compiler_params=_cparams(("parallel", "parallel", "parallel"), vm),
        name="mla_attn",
    )(q, k, km, vt, vmeta)


VIS_SINK = 2


def _bias_kernel(rb_ref, sink_ref, bk_ref, vis_ref, o_ref, *, n_buckets):
    h = pl.program_id(1)
    bk = bk_ref[0]
    vis = vis_ref[0]
    acc = jnp.zeros(bk.shape, F32)
    for j in range(n_buckets):
        acc = jnp.where(bk == j, rb_ref[j, h] * LOG2E, acc)
    o_ref[0, 0] = jnp.where(vis == VIS_SINK, sink_ref[h] * LOG2E, jnp.where(vis != 0, acc, NEG))


def _bias_tables(rel_bias, sink, bucket, vis):
    t, r, s = bucket.shape
    heads = rel_bias.shape[1]
    blk = pl.BlockSpec((1, r, s), lambda i, h: (i, 0, 0))
    smem = pl.BlockSpec(memory_space=pltpu.SMEM)
    return pl.pallas_call(
        functools.partial(_bias_kernel, n_buckets=rel_bias.shape[0]),
        grid=(t, heads),
        in_specs=[smem, smem, blk, blk],
        out_specs=pl.BlockSpec((1, 1, r, s), lambda i, h: (i, h, 0, 0)),
        out_shape=jax.ShapeDtypeStruct((t, heads, r, s), F32),
        compiler_params=_cparams(("parallel", "parallel"), 16 * r * s * 4),
        name="swa_bias_table",
    )(rel_bias, sink, bucket, vis)


def _swa_kernel(q_ref, kp_ref, ko_ref, kn_ref, km_ref, vp_ref, vo_ref, vn_ref, vm_ref, bias_ref, o_ref,
                *, kv_heads, group, n_meta):
    d = SWA_HEAD_DIM
    r = q_ref.shape[0]
    zpad = jnp.zeros((BLOCK - n_meta, d), BF16)

    for g in range(kv_heads):
        cols = slice(g * d, (g + 1) * d)
        kcat = jnp.concatenate([kp_ref[:, cols], ko_ref[:, cols], kn_ref[:, cols], km_ref[:, cols], zpad], axis=0)
        vcat = jnp.concatenate([vp_ref[:, cols], vo_ref[:, cols], vn_ref[:, cols], vm_ref[:, cols], zpad], axis=0)
        heads = range(g * group, (g + 1) * group)
        q = jnp.concatenate([q_ref[:, h * d:(h + 1) * d] for h in heads], axis=0)
        bias = jnp.concatenate([bias_ref[0, h] for h in heads], axis=0)
        s = _dot_t(q, kcat) + bias
        m = jnp.max(s, axis=-1, keepdims=True)
        acc = _dot(jnp.exp2(s - m).astype(BF16), _with_ones(vcat))
        o = acc[:, :d] / acc[:, d:]
        for j, h in enumerate(heads):
            o_ref[:, h * d:(h + 1) * d] = o[j * r:(j + 1) * r].astype(o_ref.dtype)


def _swa_attn(q_src, q_rows, kv_src, kv_meta, k_off, v_off, bias, type_of_block, kv_block_of, batch, n_grid_blk,
              n_meta):
    d = SWA_HEAD_DIM
    group = SWA_HEADS // SWA_KV_HEADS
    wq, wkv = SWA_HEADS * d, SWA_KV_HEADS * d
    nblk_kv = kv_src.shape[0] // batch // BLOCK
    s_keys = bias.shape[-1]
    assert k_off % wkv == 0 and v_off % wkv == 0
    kc, vc = k_off // wkv, v_off // wkv

    def kv_spec(which, col):
        return pl.BlockSpec((BLOCK, wkv), lambda b, i: (b * nblk_kv + kv_block_of(i)[which], col))

    def meta_spec(col):
        return pl.BlockSpec((n_meta, wkv), lambda b, i: (b, col))

    rows = group * q_rows
    vm = (2 * (2 * _nbytes((q_rows, wq), BF16) + 8 * _nbytes((BLOCK, wkv), BF16)
               + _nbytes((SWA_HEADS, q_rows, s_keys), F32))
          + SWA_KV_HEADS * (4 * _nbytes((rows, s_keys), F32) + _nbytes((rows, 2 * d), F32)))
    return pl.pallas_call(
        functools.partial(_swa_kernel, kv_heads=SWA_KV_HEADS, group=group, n_meta=n_meta),
        grid=(batch, n_grid_blk),
        in_specs=[pl.BlockSpec((q_rows, wq), lambda b, i: (b * n_grid_blk + i, 0)),
                  kv_spec(0, kc), kv_spec(1, kc), kv_spec(2, kc), meta_spec(kc),
                  kv_spec(0, vc), kv_spec(1, vc), kv_spec(2, vc), meta_spec(vc),
                  pl.BlockSpec((1, SWA_HEADS, q_rows, s_keys), lambda b, i: (type_of_block(i), 0, 0, 0))],
        out_specs=pl.BlockSpec((q_rows, wq), lambda b, i: (b * n_grid_blk + i, 0)),
        out_shape=jax.ShapeDtypeStruct((batch * n_grid_blk * q_rows, wq), BF16),
        compiler_params=_cparams(("parallel", "parallel"), vm),
        name="swa_attn",
    )(q_src, kv_src, kv_src, kv_src, kv_meta, kv_src, kv_src, kv_src, kv_meta, bias)


def _gate_proj_kernel(oa_ref, ob_ref, ga_ref, gb_ref, wa_ref, wb_ref, o_ref):
    pa = _dot(oa_ref[...], wa_ref[...])
    pb = _dot(ob_ref[...], wb_ref[...])
    ga = jax.nn.sigmoid(ga_ref[...].astype(F32))
    gb = jax.nn.sigmoid(gb_ref[...].astype(F32))
    o_ref[...] = (ga * pa + gb * pb).astype(o_ref.dtype)


def _gate_proj(oa, ob, z, ga_off, gb_off, wa, wb):
    m, ka = oa.shape
    kb = ob.shape[1]
    n = wa.shape[1]
    tm = _tile(m, 1024)
    tn = _tile(n, 512)
    assert ga_off % tn == 0 and gb_off % tn == 0
    vm = (2 * (_nbytes((tm, ka + kb), BF16) + _nbytes((ka + kb, tn), BF16) + 3 * _nbytes((tm, tn), BF16))
          + 4 * _nbytes((tm, tn), F32))
    act = lambda k: pl.BlockSpec((tm, k), lambda i, j: (i, 0))
    wsp = lambda k: pl.BlockSpec((k, tn), lambda i, j: (0, j))
    return pl.pallas_call(
        _gate_proj_kernel,
        grid=(m // tm, n // tn),
        in_specs=[act(ka), act(kb),
                  pl.BlockSpec((tm, tn), lambda i, j: (i, ga_off // tn + j)),
                  pl.BlockSpec((tm, tn), lambda i, j: (i, gb_off // tn + j)),
                  wsp(ka), wsp(kb)],
        out_specs=pl.BlockSpec((tm, tn), lambda i, j: (i, j)),
        out_shape=jax.ShapeDtypeStruct((m, n), BF16),
        compiler_params=_cparams(("parallel", "parallel"), vm),
        name="gate_proj",
    )(oa, ob, z, z, wa, wb)


def _out_ln_kernel(m_ref, w_ref, h_ref, g_ref, b_ref, ho_ref, *, alpha):
    y = alpha * h_ref[...] + _dot(m_ref[...], w_ref[...])
    ho_ref[...] = _ln_rows(y, g_ref[...], b_ref[...])


def _out_ln(merged, w, h, g, b, alpha):
    m, d = h.shape
    tm = _tile(m, 512)
    row = pl.BlockSpec((tm, d), lambda i: (i, 0))
    vec = pl.BlockSpec((1, d), lambda i: (0, 0))
    vm = 2 * (_nbytes((tm, d), BF16) + _nbytes(w.shape, BF16) + 2 * _nbytes((tm, d), F32)) + 3 * _nbytes((tm, d), F32)
    return pl.pallas_call(
        functools.partial(_out_ln_kernel, alpha=alpha),
        grid=(m // tm,),
        in_specs=[row, pl.BlockSpec(w.shape, lambda i: (0, 0)), row, vec, vec],
        out_specs=row,
        out_shape=jax.ShapeDtypeStruct((m, d), F32),
        compiler_params=_cparams(("parallel",), vm),
        name="out_proj_ln",
    )(merged, w, h, g.reshape(1, d), b.reshape(1, d))


def _swiglu(a1, a3):
    return a1 * jax.nn.sigmoid(a1) * a3


def _ffn_ln_kernel(h_ref, w1_ref, w3_ref, w2_ref, lg_ref, lb_ref, ho_ref, hbo_ref, xb_ref, acc_ref, *, alpha):
    j = pl.program_id(1)

    @pl.when(j == 0)
    def _():
        xb_ref[...] = h_ref[...].astype(BF16)
        acc_ref[...] = jnp.zeros_like(acc_ref)

    xb = xb_ref[...]
    gact = _swiglu(_dot(xb, w1_ref[...]), _dot(xb, w3_ref[...])).astype(BF16)
    acc_ref[...] += _dot(gact, w2_ref[...])

    @pl.when(j == pl.num_programs(1) - 1)
    def _():
        y = _ln_rows(alpha * h_ref[...] + acc_ref[...], lg_ref[...], lb_ref[...])
        ho_ref[...] = y
        hbo_ref[...] = y.astype(BF16)


def _ffn_ln(h, w1, w3, w2, lg, lb, alpha):
    m, d = h.shape
    f = w1.shape[1]
    tm = _tile(m, 512)
    tf = _tile(f, 512)
    vm = (2 * (2 * _nbytes((tm, d), F32) + 3 * _nbytes((d, tf), BF16) + _nbytes((tm, d), BF16))
          + _nbytes((tm, d), BF16) + _nbytes((tm, d), F32) + 4 * _nbytes((tm, tf), F32) + 2 * _nbytes((tm, d), F32))
    row = pl.BlockSpec((tm, d), lambda i, j: (i, 0))
    vec = pl.BlockSpec((1, d), lambda i, j: (0, 0))
    wup = pl.BlockSpec((d, tf), lambda i, j: (0, j))
    return pl.pallas_call(
        functools.partial(_ffn_ln_kernel, alpha=alpha),
        grid=(m // tm, f // tf),
        in_specs=[row, wup, wup, pl.BlockSpec((tf, d), lambda i, j: (j, 0)), vec, vec],
        out_specs=[row, row],
        out_shape=[jax.ShapeDtypeStruct((m, d), F32), jax.ShapeDtypeStruct((m, d), BF16)],
        scratch_shapes=[pltpu.VMEM((tm, d), BF16), pltpu.VMEM((tm, d), F32)],
        compiler_params=_cparams(("parallel", "arbitrary"), vm),
        name="ffn_ln",
    )(h, w1, w3, w2, lg.reshape(1, d), lb.reshape(1, d))


_PK_E1, _PK_E2, _PK_G1, _PK_G2, _PK_R1, _PK_R2 = range(6)


def _router_kernel(h_ref, w_ref, b_ref, pk_ref, cnt_ref, carry_ref, *, n_exp):
    @pl.when(pl.program_id(0) == 0)
    def _():
        carry_ref[...] = jnp.zeros_like(carry_ref)

    logits = jnp.dot(h_ref[...], w_ref[...], preferred_element_type=F32, precision=lax.Precision.HIGHEST) + b_ref[...]
    tm, nl = logits.shape
    lane = lax.broadcasted_iota(jnp.int32, (tm, nl), 1).astype(F32)
    s = jnp.where(lane < n_exp, logits, -jnp.inf)
    m1 = jnp.max(s, axis=-1, keepdims=True)
    i1 = jnp.min(jnp.where(s == m1, lane, float(nl)), axis=-1, keepdims=True)
    s2 = jnp.where(lane == i1, -jnp.inf, s)
    m2 = jnp.max(s2, axis=-1, keepdims=True)
    i2 = jnp.min(jnp.where(s2 == m2, lane, float(nl)), axis=-1, keepdims=True)
    e = jnp.exp(m2 - m1)
    g1 = 1.0 / (1.0 + e)
    g2 = e / (1.0 + e)
    oh1 = lane == i1
    oh2 = lane == i2
    both = jnp.where(oh1 | oh2, 1.0, 0.0)
    r = lax.broadcasted_iota(jnp.int32, (tm, tm), 0)
    c = lax.broadcasted_iota(jnp.int32, (tm, tm), 1)
    tri = jnp.where(c < r, 1.0, 0.0).astype(BF16)
    prefix = _dot(tri, both.astype(BF16)) + carry_ref[...]
    r1 = jnp.sum(jnp.where(oh1, prefix, 0.0), axis=-1, keepdims=True)
    r2 = jnp.sum(jnp.where(oh2, prefix, 0.0), axis=-1, keepdims=True)
    carry_ref[...] += jnp.sum(both, axis=0, keepdims=True)
    cnt_ref[...] = carry_ref[...]
    pk = jnp.zeros((tm, nl), F32)
    for col, val in ((_PK_E1, i1), (_PK_E2, i2), (_PK_G1, g1), (_PK_G2, g2), (_PK_R1, r1), (_PK_R2, r2)):
        pk = jnp.where(lane == col, val, pk)
    pk_ref[...] = pk


def _router(h, w_pad, b_pad, n_exp):
    m, d = h.shape
    nl = w_pad.shape[1]
    tm = _tile(m, 512)
    vm = 2 * (_nbytes((tm, d), F32) + _nbytes((d, nl), F32) + _nbytes((tm, nl), F32)) + _nbytes((tm, tm), F32) * 3
    return pl.pallas_call(
        functools.partial(_router_kernel, n_exp=n_exp),
        grid=(m // tm,),
        in_specs=[pl.BlockSpec((tm, d), lambda i: (i, 0)), pl.BlockSpec((d, nl), lambda i: (0, 0)),
                  pl.BlockSpec((1, nl), lambda i: (0, 0))],
        out_specs=[pl.BlockSpec((tm, nl), lambda i: (i, 0)), pl.BlockSpec((1, nl), lambda i: (0, 0))],
        out_shape=[jax.ShapeDtypeStruct((m, nl), F32), jax.ShapeDtypeStruct((1, nl), F32)],
        scratch_shapes=[pltpu.VMEM((1, nl), F32)],
        compiler_params=_cparams(("arbitrary",), vm),
        name="moe_router",
    )(h, w_pad, b_pad)


def _row_copy(src_ref, src_row, dst_ref, dst_row, sem):
    return pltpu.make_async_copy(src_ref.at[pl.ds(src_row, 1)], dst_ref.at[pl.ds(dst_row, 1)], sem)


DMA_ISSUE_UNROLL = 8


def _dispatch_kernel(d1_ref, d2_ref, pad_ref, h_ref, zero_ref, xs_ref, sem, *, n_ranges):
    tb = h_ref.shape[0]

    def issue(t, carry):
        _row_copy(h_ref, t, xs_ref, d1_ref[0, 0, t], sem).start()
        _row_copy(h_ref, t, xs_ref, d2_ref[0, 0, t], sem).start()
        return carry

    lax.fori_loop(0, tb, issue, 0, unroll=DMA_ISSUE_UNROLL)

    @pl.when(pl.program_id(0) == 0)
    def _():
        for e in range(n_ranges):
            base = pad_ref[0, e]
            n = pad_ref[1, e]
            lax.fori_loop(0, n, lambda j, c: (_row_copy(zero_ref, 0, xs_ref, base + j, sem).start(), c)[1], 0)
            lax.fori_loop(0, n, lambda j, c: (_row_copy(zero_ref, 0, xs_ref, 0, sem).wait(), c)[1], 0)

    for _ in range(TOP_K):
        pltpu.make_async_copy(h_ref, xs_ref.at[pl.ds(0, tb)], sem).wait()


def _dispatch(h, d1, d2, pad_info, n_slots):
    m, d = h.shape
    tb = _tile(m, 512)
    nt = m // tb
    idx = pl.BlockSpec((1, 1, tb), lambda i: (i, 0, 0), memory_space=pltpu.SMEM)
    zero = jnp.zeros((8, d), F32)
    return pl.pallas_call(
        functools.partial(_dispatch_kernel, n_ranges=pad_info.shape[1]),
        grid=(nt,),
        in_specs=[idx, idx, pl.BlockSpec(memory_space=pltpu.SMEM), pl.BlockSpec((tb, d), lambda i: (i, 0)),
                  pl.BlockSpec((8, d), lambda i: (0, 0))],
        out_specs=pl.BlockSpec(memory_space=pl.ANY),
        out_shape=jax.ShapeDtypeStruct((n_slots, d), F32),
        scratch_shapes=[pltpu.SemaphoreType.DMA(())],
        compiler_params=_cparams(("arbitrary",), 4 * tb * d * 4),
        name="moe_dispatch",
    )(d1.reshape(nt, 1, tb), d2.reshape(nt, 1, tb), pad_info, h, zero)


def _moe_ffn_kernel(te_ref, ts_ref, tv_ref, x_ref, w1_ref, w3_ref, w2_ref, y_ref, xb_ref):
    i = pl.program_id(0)
    j = pl.program_id(1)

    @pl.when(tv_ref[i] != 0)
    def _():
        @pl.when(j == 0)
        def _():
            xb_ref[...] = x_ref[...].astype(BF16)
            y_ref[...] = jnp.zeros_like(y_ref)

        xb = xb_ref[...]
        gact = _swiglu(_dot(xb, w1_ref[0]), _dot(xb, w3_ref[0])).astype(BF16)
        y_ref[...] += _dot(gact, w2_ref[0])

    @pl.when((tv_ref[i] == 0) & (j == 0))
    def _():
        y_ref[...] = jnp.zeros_like(y_ref)


def _moe_ffn(xs, w1, w3, w2, tile_expert, tile_src, tile_valid, tm):
    n_slots, d = xs.shape
    f = w1.shape[2]
    tf = _tile(f, 512)
    nf = f // tf
    n_tiles = n_slots // tm

    def jj(i, j, tv):
        return jnp.where(tv[i] != 0, j, nf - 1)

    vm = (2 * (2 * _nbytes((tm, d), F32) + 3 * _nbytes((d, tf), BF16)) + _nbytes((tm, d), BF16)
          + 4 * _nbytes((tm, tf), F32) + _nbytes((tm, d), F32))
    grid_spec = pltpu.PrefetchScalarGridSpec(
        num_scalar_prefetch=3,
        grid=(n_tiles, nf),
        in_specs=[pl.BlockSpec((tm, d), lambda i, j, te, ts, tv: (ts[i], 0)),
                  pl.BlockSpec((1, d, tf), lambda i, j, te, ts, tv: (te[i], 0, jj(i, j, tv))),
                  pl.BlockSpec((1, d, tf), lambda i, j, te, ts, tv: (te[i], 0, jj(i, j, tv))),
                  pl.BlockSpec((1, tf, d), lambda i, j, te, ts, tv: (te[i], jj(i, j, tv), 0))],
        out_specs=pl.BlockSpec((tm, d), lambda i, j, te, ts, tv: (i, 0)),
        scratch_shapes=[pltpu.VMEM((tm, d), BF16)],
    )
    return pl.pallas_call(
        _moe_ffn_kernel,
        grid_spec=grid_spec,
        out_shape=jax.ShapeDtypeStruct((n_slots, d), F32),
        compiler_params=_cparams(("arbitrary", "arbitrary"), vm),
        name="moe_expert_ffn",
    )(tile_expert, tile_src, tile_valid, xs, w1, w3, w2)


def _combine_ln_kernel(d1_ref, d2_ref, d1n_ref, d2n_ref, pk_ref, h_ref, y_ref, lg_ref, lb_ref, o_ref, ybuf_ref, sems,
                       *, alpha, tb):
    step = pl.program_id(0)

    def gather(i1_ref, i2_ref, off, buf):
        def issue(t, carry):
            _row_copy(y_ref, i1_ref[0, 0, off + t], ybuf_ref.at[buf, 0], t, sems.at[buf]).start()
            _row_copy(y_ref, i2_ref[0, 0, off + t], ybuf_ref.at[buf, 1], t, sems.at[buf]).start()
            return carry

        lax.fori_loop(0, tb, issue, 0, unroll=DMA_ISSUE_UNROLL)

    def wait(buf):
        for k in range(TOP_K):
            pltpu.make_async_copy(y_ref.at[pl.ds(0, tb)], ybuf_ref.at[buf, k], sems.at[buf]).wait()

    def finish(buf, rows):
        pk = pk_ref[rows, :]
        f = pk[:, _PK_G1:_PK_G1 + 1] * ybuf_ref[buf, 0] + pk[:, _PK_G2:_PK_G2 + 1] * ybuf_ref[buf, 1]
        o_ref[rows, :] = _ln_rows(alpha * h_ref[rows, :] + f, lg_ref[...], lb_ref[...])

    @pl.when(step == 0)
    def _():
        gather(d1_ref, d2_ref, 0, 0)

    gather(d1_ref, d2_ref, tb, 1)
    wait(0)
    finish(0, slice(0, tb))

    @pl.when(step < pl.num_programs(0) - 1)
    def _():
        gather(d1n_ref, d2n_ref, 0, 0)

    wait(1)
    finish(1, slice(tb, 2 * tb))


def _combine_ln(h, pk, ys, d1, d2, lg, lb, alpha):
    m, d = h.shape
    tb = _tile(m // 2, 256)
    ns = m // (2 * tb)
    idx = pl.BlockSpec((1, 1, 2 * tb), lambda i: (i, 0, 0), memory_space=pltpu.SMEM)
    idx_next = pl.BlockSpec((1, 1, 2 * tb), lambda i: (jnp.minimum(i + 1, ns - 1), 0, 0), memory_space=pltpu.SMEM)
    row = pl.BlockSpec((2 * tb, d), lambda i: (i, 0))
    vec = pl.BlockSpec((1, d), lambda i: (0, 0))
    vm = (2 * (2 * _nbytes((2 * tb, d), F32) + _nbytes((2 * tb, pk.shape[1]), F32)) + 4 * _nbytes((tb, d), F32)
          + 4 * _nbytes((tb, d), F32))
    d1b, d2b = d1.reshape(ns, 1, 2 * tb), d2.reshape(ns, 1, 2 * tb)
    return pl.pallas_call(
        functools.partial(_combine_ln_kernel, alpha=alpha, tb=tb),
        grid=(ns,),
        in_specs=[idx, idx, idx_next, idx_next, pl.BlockSpec((2 * tb, pk.shape[1]), lambda i: (i, 0)), row,
                  pl.BlockSpec(memory_space=pl.ANY), vec, vec],
        out_specs=row,
        out_shape=jax.ShapeDtypeStruct((m, d), F32),
        scratch_shapes=[pltpu.VMEM((2, TOP_K, tb, d), F32), pltpu.SemaphoreType.DMA((2,))],
        compiler_params=_cparams(("arbitrary",), vm),
        name="moe_combine_ln",
    )(d1b, d2b, d1b, d2b, pk, h, ys, lg.reshape(1, d), lb.reshape(1, d))


def _rope_tables(n_tok):
    pos = jnp.arange(n_tok, dtype=F32)
    inv = ROPE_THETA ** (-jnp.arange(0, MLA_ROPE_DIM, 2, dtype=F32) / MLA_ROPE_DIM)
    ang = pos[:, None] * inv[None, :]
    cos, sin = jnp.cos(ang), jnp.sin(ang)
    zero = jnp.zeros_like(cos)
    return jnp.concatenate([cos, cos, zero, zero], -1), jnp.concatenate([-sin, sin, zero, zero], -1)


def _rel_bucket(rel):
    nb = N_BUCKETS // 2
    max_exact = nb // 2
    n = jnp.abs(rel)
    large = max_exact + (jnp.log(jnp.maximum(n, 1).astype(F32) / max_exact)
                         / math.log(MAX_DISTANCE / max_exact) * (nb - max_exact)).astype(jnp.int32)
    large = jnp.minimum(large, nb - 1)
    return jnp.where(rel > 0, nb, 0) + jnp.where(n < max_exact, n, large)


def _swa_index_tables(n_meta, n_real):
    nblk = n_real // BLOCK
    band = jnp.arange(3 * BLOCK)
    meta_pos = jnp.arange(n_meta)
    pad = BLOCK - n_meta

    def one(start, q_pos, first_block_only=False):
        r_key = start - BLOCK + band
        in_range = (r_key >= 0) & (r_key < n_real)
        if first_block_only:
            in_range = in_range & (r_key < BLOCK)
        k_pos = jnp.concatenate([n_meta + r_key, meta_pos, jnp.zeros((pad,), jnp.int32)])
        always = jnp.concatenate([jnp.zeros((3 * BLOCK,), bool), jnp.ones((n_meta,), bool), jnp.zeros((pad,), bool)])
        live = jnp.concatenate([in_range, jnp.ones((n_meta,), bool), jnp.zeros((pad,), bool)])
        rel = k_pos[None, :] - q_pos[:, None]
        vis = always[None, :] | (live[None, :] & (jnp.abs(rel) <= WINDOW))
        return _rel_bucket(rel), vis

    q_local = jnp.arange(BLOCK)
    tabs = [one(0, n_meta + q_local),
            one(BLOCK * min(1, nblk - 1), n_meta + BLOCK * min(1, nblk - 1) + q_local),
            one(BLOCK * (nblk - 1), n_meta + BLOCK * (nblk - 1) + q_local),
            one(0, jnp.where(q_local < n_meta, q_local, 0), first_block_only=True)]
    bucket = jnp.stack([t[0] for t in tabs]).astype(jnp.int32)
    vis = jnp.stack([t[1] for t in tabs]).astype(jnp.int32)
    vis = vis.at[:, :, 3 * BLOCK + n_meta].set(VIS_SINK)
    return bucket, vis


def _layer_weights(l, w_in, w_uq, w_ukv, w_proj_a, w_proj_b, w_out, d_model, q_rank, kv_rank):
    hd = SWA_HEADS * SWA_HEAD_DIM
    kvd = SWA_KV_HEADS * SWA_HEAD_DIM
    splits = (q_rank, kv_rank, MLA_ROPE_DIM, hd, kvd, kvd, d_model, d_model)
    off = [0]
    for s in splits:
        off.append(off[-1] + s)
    swa_q_scale = SWA_HEAD_DIM ** -0.5 * LOG2E
    col = lambda i: w_in[l][:, off[i]:off[i + 1]] * (swa_q_scale if i == 3 else 1.0)
    order = (3, 6, 7, 1, 4, 5, 0)
    w_z = jnp.concatenate([col(i) for i in order], axis=1).astype(BF16)
    z_off = {}
    o = 0
    for i in order:
        z_off[i] = o
        o += splits[i]
    half = MLA_ROPE_DIM // 2
    kr = col(2)
    w_kr = jnp.concatenate([kr[:, :half], kr[:, half:], kr[:, :half], kr[:, half:]], axis=1).astype(BF16)
    uq = w_uq[l].reshape(q_rank, MLA_HEADS, MLA_NOPE_DIM + MLA_ROPE_DIM)
    uq_n = uq[:, :, :MLA_NOPE_DIM].reshape(q_rank, MLA_HEADS * MLA_NOPE_DIM).astype(BF16)
    r1 = uq[:, :, MLA_NOPE_DIM:MLA_NOPE_DIM + half]
    r2 = uq[:, :, MLA_NOPE_DIM + half:]
    uq_r = jnp.concatenate([r1, r2, r1, r2], axis=2).reshape(q_rank, MLA_HEADS * V7X_LANES).astype(BF16)
    ukv = w_ukv[l].reshape(kv_rank, MLA_HEADS, MLA_NOPE_DIM + MLA_V_DIM)
    uk = ukv[:, :, :MLA_NOPE_DIM].reshape(kv_rank, MLA_HEADS * MLA_NOPE_DIM).astype(BF16)
    uv = ukv[:, :, MLA_NOPE_DIM:].reshape(kv_rank, MLA_HEADS * MLA_V_DIM).astype(BF16)
    return dict(w_z=w_z, z_off=z_off, w_kr=w_kr, uq_n=uq_n, uq_r=uq_r, uk=uk, uv=uv, uv_t=uv.T,
                wa=w_proj_a[l].astype(BF16), wb=w_proj_b[l].astype(BF16), wo=w_out[l].astype(BF16))


def kernel(x, meta_tokens, emb_ln_g, emb_ln_b, rel_bias, w_in, q_norm_g, kv_norm_g, w_uq, w_ukv, sink_logits,
           w_proj_a, w_proj_b, w_out, ln_mix_g, ln_mix_b, ln_ffn_g, ln_ffn_b, ffn_w1, ffn_w3, ffn_w2, router_w,
           router_b, moe_w1, moe_w3, moe_w2):
    bsz, seq, d = x.shape
    n_meta = meta_tokens.shape[0]
    depth = w_in.shape[0]
    q_rank = q_norm_g.shape[1]
    kv_rank = kv_norm_g.shape[1]
    alpha = (2 * depth) ** 0.25
    nblk = seq // BLOCK
    assert seq % BLOCK == 0 and n_meta % 16 == 0 and n_meta <= BLOCK

    hr, hr_b = _embed_ln(x.reshape(bsz * seq, d), emb_ln_g, emb_ln_b)
    hm1, hm1_b = _embed_ln(meta_tokens.astype(x.dtype), emb_ln_g, emb_ln_b)
    hm, hm_b = jnp.tile(hm1, (bsz, 1)), jnp.tile(hm1_b, (bsz, 1))

    ct, st = _rope_tables(n_meta + seq)
    ct_r, st_r = ct[n_meta:], st[n_meta:]
    ct_m, st_m = jnp.tile(ct[:n_meta], (bsz, 1)), jnp.tile(st[:n_meta], (bsz, 1))
    bucket, vis = _swa_index_tables(n_meta, seq)

    last_blk = nblk - 1
    type_real = lambda i: jnp.where(i == 0, 0, jnp.where(i == last_blk, 2, 1))
    kv_real = lambda i: (jnp.maximum(i - 1, 0), i, jnp.minimum(i + 1, last_blk))
    type_meta = lambda i: 0
    kv_meta_q = lambda i: (0, 0, 0)

    for l in range(depth):
        last = l == depth - 1
        lw = _layer_weights(l, w_in, w_uq, w_ukv, w_proj_a, w_proj_b, w_out, d, q_rank, kv_rank)
        zo = lw["z_off"]
        bias_all = _bias_tables(rel_bias, sink_logits[l], bucket, vis)
        bias_real = bias_all[:3]
        bias_meta = bias_all[3:, :, :n_meta]

        z_r = _matmul(hr_b, lw["w_z"], BF16)
        z_m = _matmul(hm_b, lw["w_z"], BF16)
        k_r, v_r, ksq_r = _kv_proj(hr_b, z_r, zo[1], kv_norm_g[l], lw["uk"], lw["uv_t"], lw["w_kr"], ct_r, st_r, True)
        k_m, v_m, ksq_m = _kv_proj(hm_b, z_m, zo[1], kv_norm_g[l], lw["uk"], lw["uv"], lw["w_kr"], ct_m, st_m, False)
        kmax = jnp.maximum(ksq_r.reshape(bsz, seq, -1).max(axis=1), ksq_m.reshape(bsz, n_meta, -1).max(axis=1))
        kmax_r, kmax_m = jnp.repeat(kmax, seq, axis=0), jnp.repeat(kmax, n_meta, axis=0)
        q_r = _q_proj(z_r, zo[0], q_norm_g[l], lw["uq_n"], lw["uq_r"], ct_r, st_r, kmax_r)
        oa_r = _mla_attn(q_r, k_r, k_m, v_r, v_m, bsz, n_meta)
        ob_r = _swa_attn(z_r, BLOCK, z_r, z_m, zo[4], zo[5], bias_real, type_real, kv_real, bsz, nblk, n_meta)
        mg_r = _gate_proj(oa_r, ob_r, z_r, zo[6], zo[7], lw["wa"], lw["wb"])
        hr = _out_ln(mg_r, lw["wo"], hr, ln_mix_g[l], ln_mix_b[l], alpha)
        if not last:
            q_m = _q_proj(z_m, zo[0], q_norm_g[l], lw["uq_n"], lw["uq_r"], ct_m, st_m, kmax_m)
            oa_m = _mla_attn(q_m, k_r, k_m, v_r, v_m, bsz, n_meta)
            ob_m = _swa_attn(z_m, n_meta, z_r, z_m, zo[4], zo[5], bias_meta, type_meta, kv_meta_q, bsz, 1, n_meta)
            mg_m = _gate_proj(oa_m, ob_m, z_m, zo[6], zo[7], lw["wa"], lw["wb"])
            hm = _out_ln(mg_m, lw["wo"], hm, ln_mix_g[l], ln_mix_b[l], alpha)

        if l % 2 == 0:
            w1, w3, w2 = (ffn_w1[l // 2].astype(BF16), ffn_w3[l // 2].astype(BF16), ffn_w2[l // 2].astype(BF16))
            hr, hr_b = _ffn_ln(hr, w1, w3, w2, ln_ffn_g[l], ln_ffn_b[l], alpha)
            if not last:
                hm, hm_b = _ffn_ln(hm, w1, w3, w2, ln_ffn_g[l], ln_ffn_b[l], alpha)
        else:
            hr = _moe_layer(hr, router_w[l // 2], router_b[l // 2], moe_w1[l // 2], moe_w3[l // 2], moe_w2[l // 2],
                            ln_ffn_g[l], ln_ffn_b[l], alpha)
            hr_b = hr.astype(BF16)
            if not last:
                hm = _moe_layer(hm, router_w[l // 2], router_b[l // 2], moe_w1[l // 2], moe_w3[l // 2],
                                moe_w2[l // 2], ln_ffn_g[l], ln_ffn_b[l], alpha)
                hm_b = hm.astype(BF16)
    return hr.reshape(bsz, seq, d)


def _moe_layer(h, router_w, router_b, w1, w3, w2, lg, lb, alpha):
    m, d = h.shape
    n_exp = router_w.shape[1]
    tm = _tile(m, 512)
    w_pad = jnp.zeros((d, V7X_LANES), F32).at[:, :n_exp].set(router_w)
    b_pad = jnp.zeros((1, V7X_LANES), F32).at[0, :n_exp].set(router_b)
    pk, cnt = _router(h, w_pad, b_pad, n_exp)
    counts = cnt[0, :n_exp].astype(jnp.int32)
    padded = (counts + tm - 1) // tm * tm
    pad_end = jnp.cumsum(padded)
    pad_start = pad_end - padded
    e1 = pk[:, _PK_E1].astype(jnp.int32)
    e2 = pk[:, _PK_E2].astype(jnp.int32)
    d1 = pad_start[e1] + pk[:, _PK_R1].astype(jnp.int32)
    d2 = pad_start[e2] + pk[:, _PK_R2].astype(jnp.int32)
    n_tiles = (m * TOP_K) // tm + n_exp
    n_slots = n_tiles * tm
    n_used = pad_end[-1] // tm
    tile_id = jnp.arange(n_tiles, dtype=jnp.int32)
    tile_valid = (tile_id < n_used).astype(jnp.int32)
    tile_src = jnp.minimum(tile_id, n_used - 1).astype(jnp.int32)
    tile_expert = jnp.minimum(jnp.searchsorted(pad_end, tile_src * tm, side="right"), n_exp - 1).astype(jnp.int32)
    pad_info = jnp.stack([jnp.append(pad_start + counts, pad_end[-1]),
                          jnp.append(padded - counts, n_slots - pad_end[-1])]).astype(jnp.int32)

    xs = _dispatch(h, d1, d2, pad_info, n_slots)
    ys = _moe_ffn(xs, w1.astype(BF16), w3.astype(BF16), w2.astype(BF16), tile_expert, tile_src, tile_valid, tm)
    return _combine_ln(h, pk, ys, d1, d2, lg, lb, alpha)
```

```python
import functools
import math

import jax
import jax.numpy as jnp
from jax import lax
from jax.experimental import pallas as pl
from jax.experimental.pallas import tpu as pltpu

MLA_HEADS = 16
MLA_NOPE_DIM = 128
MLA_ROPE_DIM = 64
MLA_V_DIM = 128
SWA_HEADS = 16
SWA_KV_HEADS = 4
SWA_HEAD_DIM = 128
WINDOW = 128
BLOCK = 128
N_BUCKETS = 32
MAX_DISTANCE = 128
TOP_K = 2
ROPE_THETA = 10000.0
LN_EPS = 1e-5
RMS_EPS = 1e-6
NEG = -1e30
LOG2E = math.log2(math.e)

V7X_LANES = 128
V7X_MXU_DIM = 256
V7X_VMEM_LIMIT_BYTES = 56 * 1024 * 1024

F32 = jnp.float32
BF16 = jnp.bfloat16


COMPILER_SCRATCH_BYTES = 8 * 1024 * 1024


def _cparams(semantics, vmem_bytes):
    limit = int(min(V7X_VMEM_LIMIT_BYTES, vmem_bytes + COMPILER_SCRATCH_BYTES))
    return pltpu.CompilerParams(dimension_semantics=semantics, vmem_limit_bytes=limit)


def _nbytes(shape, dtype):
    return math.prod(shape) * jnp.dtype(dtype).itemsize


def _dot(a, b):
    return jnp.dot(a, b, preferred_element_type=F32)


def _dot_t(a, b):
    return lax.dot_general(a, b, (((1,), (1,)), ((), ())), preferred_element_type=F32)


def _tile(n, pref):
    t = min(n, pref)
    while n % t:
        t //= 2
    return t


def _ln_rows(y, g, b):
    mu = jnp.mean(y, axis=-1, keepdims=True)
    d = y - mu
    var = jnp.mean(d * d, axis=-1, keepdims=True)
    return d * lax.rsqrt(var + LN_EPS) * g + b


def _embed_ln_kernel(x_ref, g_ref, b_ref, h_ref, hb_ref):
    y = _ln_rows(x_ref[...], g_ref[...], b_ref[...])
    h_ref[...] = y
    hb_ref[...] = y.astype(BF16)


def _embed_ln(x, g, b):
    m, d = x.shape
    tm = _tile(m, 512)
    row = pl.BlockSpec((tm, d), lambda i: (i, 0))
    vec = pl.BlockSpec((1, d), lambda i: (0, 0))
    return pl.pallas_call(
        _embed_ln_kernel,
        grid=(m // tm,),
        in_specs=[row, vec, vec],
        out_specs=[row, row],
        out_shape=[jax.ShapeDtypeStruct((m, d), F32), jax.ShapeDtypeStruct((m, d), BF16)],
        compiler_params=_cparams(("parallel",), 2 * tm * d * 10 + 4 * tm * d * 4),
        name="embed_ln",
    )(x, g.reshape(1, d), b.reshape(1, d))


def _mm_kernel(a_ref, w_ref, o_ref):
    o_ref[...] = _dot(a_ref[...], w_ref[...]).astype(o_ref.dtype)


def _matmul(a, w, out_dtype, tm_pref=2048, tn_pref=768):
    m, k = a.shape
    n = w.shape[1]
    tm = _tile(m, tm_pref)
    tn = next(t for t in (tn_pref, 512, 256, 128) if n % t == 0)
    vm = 2 * (_nbytes((tm, k), BF16) + _nbytes((k, tn), BF16) + _nbytes((tm, tn), out_dtype)) + _nbytes((tm, tn), F32)
    return pl.pallas_call(
        _mm_kernel,
        grid=(m // tm, n // tn),
        in_specs=[pl.BlockSpec((tm, k), lambda i, j: (i, 0)), pl.BlockSpec((k, tn), lambda i, j: (0, j))],
        out_specs=pl.BlockSpec((tm, tn), lambda i, j: (i, j)),
        out_shape=jax.ShapeDtypeStruct((m, n), out_dtype),
        compiler_params=_cparams(("parallel", "parallel"), vm),
        name="in_proj",
    )(a, w)


def _rope(x, ct, st):
    return x * ct + pltpu.roll(x, MLA_ROPE_DIM // 2, 1) * st


def _rms(c, g):
    return c * lax.rsqrt(jnp.mean(c * c, axis=-1, keepdims=True) + RMS_EPS) * g


SHIFT_LANE = MLA_ROPE_DIM
BOUND_MARGIN = 1.0 + 2.0 ** -7


def _sq_norm(*parts):
    sq = sum(jnp.square(p.astype(F32)) for p in parts)
    return jnp.sum(sq, axis=-1, keepdims=True)


def _q_proj_kernel(c_ref, g_ref, wn_ref, wr_ref, ct_ref, st_ref, km_ref, hsum_ref, hput_ref, o_ref, *, heads, scale):
    cb = _rms(c_ref[...].astype(F32), g_ref[...]).astype(BF16)
    ct = ct_ref[...]
    st = st_ref[...]
    qn = (_dot(cb, wn_ref[...]) * scale).astype(BF16)
    qr_f = _dot(cb, wr_ref[...])
    qr = jnp.concatenate([(_rope(qr_f[:, h * V7X_LANES:(h + 1) * V7X_LANES], ct, st) * scale).astype(BF16)
                          for h in range(heads)], axis=1)
    sq = (jnp.square(qn.astype(F32)) + jnp.square(qr.astype(F32))).astype(BF16)
    bound = jnp.sqrt(_dot(sq, hsum_ref[...]) * km_ref[...]) * BOUND_MARGIN
    shift = _dot((-bound).astype(BF16), hput_ref[...])
    qr = (qr.astype(F32) + shift).astype(BF16)
    w = MLA_NOPE_DIM + V7X_LANES
    for h in range(heads):
        o_ref[:, h * w:h * w + MLA_NOPE_DIM] = qn[:, h * MLA_NOPE_DIM:(h + 1) * MLA_NOPE_DIM]
        o_ref[:, h * w + MLA_NOPE_DIM:(h + 1) * w] = qr[:, h * V7X_LANES:(h + 1) * V7X_LANES]


def _q_proj(z, cq_off, g, wn, wr, ct, st, kmax_rows):
    m = z.shape[0]
    r = g.shape[0]
    heads = MLA_HEADS
    tm = _tile(math.gcd(m, ct.shape[0]), 512)
    npos = ct.shape[0] // tm
    assert cq_off % r == 0
    wout = heads * (MLA_NOPE_DIM + V7X_LANES)
    scale = (MLA_NOPE_DIM + MLA_ROPE_DIM) ** -0.5 * LOG2E
    vm = (2 * (_nbytes((tm, r), BF16) + _nbytes(wn.shape, BF16) + _nbytes(wr.shape, BF16) + 2 * tm * V7X_LANES * 4
               + _nbytes((tm, wout), BF16)) + 2 * _nbytes((tm, wn.shape[1]), F32) + _nbytes((tm, r), F32) * 2)
    tab = pl.BlockSpec((tm, V7X_LANES), lambda i: (i % npos, 0))
    assert MLA_NOPE_DIM == V7X_LANES and heads <= V7X_LANES
    chan = jnp.arange(heads * V7X_LANES)
    head_id = jnp.arange(V7X_LANES)
    hsum = (chan[:, None] // V7X_LANES == head_id[None, :]).astype(BF16)
    hput = (head_id[:, None] * V7X_LANES + SHIFT_LANE == chan[None, :]).astype(BF16)
    const = lambda a: pl.BlockSpec(a.shape, lambda i: (0, 0))
    return pl.pallas_call(
        functools.partial(_q_proj_kernel, heads=heads, scale=scale),
        grid=(m // tm,),
        in_specs=[pl.BlockSpec((tm, r), lambda i: (i, cq_off // r)),
                  pl.BlockSpec((1, r), lambda i: (0, 0)),
                  const(wn), const(wr),
                  tab, tab, pl.BlockSpec((tm, V7X_LANES), lambda i: (i, 0)), const(hsum), const(hput)],
        out_specs=pl.BlockSpec((tm, wout), lambda i: (i, 0)),
        out_shape=jax.ShapeDtypeStruct((m, wout), BF16),
        compiler_params=_cparams(("parallel",), vm),
        name="mla_q_proj",
    )(z, g.reshape(1, r), wn, wr, ct, st, kmax_rows, hsum, hput)


def _kv_proj_kernel(hb_ref, c_ref, g_ref, wk_ref, wv_ref, wkr_ref, ct_ref, st_ref, k_ref, v_ref, ksq_ref, *, heads,
                    transpose_v):
    cb = _rms(c_ref[...].astype(F32), g_ref[...]).astype(BF16)
    kn = _dot(cb, wk_ref[...])
    if transpose_v:
        v_ref[...] = _dot_t(wv_ref[...], cb).astype(BF16)
    else:
        v_ref[...] = _dot(cb, wv_ref[...]).astype(BF16)
    kr = _rope(_dot(hb_ref[...], wkr_ref[...]), ct_ref[...], st_ref[...]).astype(BF16)
    lane = lax.broadcasted_iota(jnp.int32, kr.shape, 1)
    kr_one = jnp.where(lane == SHIFT_LANE, 1.0, kr).astype(BF16)
    kr_sq = _sq_norm(kr)
    ksq = jnp.zeros(kr.shape, F32)
    w = MLA_NOPE_DIM + V7X_LANES
    for h in range(heads):
        kn_h = kn[:, h * MLA_NOPE_DIM:(h + 1) * MLA_NOPE_DIM].astype(BF16)
        k_ref[:, h * w:h * w + MLA_NOPE_DIM] = kn_h
        k_ref[:, h * w + MLA_NOPE_DIM:(h + 1) * w] = kr_one
        ksq = jnp.where(lane == h, _sq_norm(kn_h) + kr_sq, ksq)
    ksq_ref[...] = ksq


def _kv_proj(hb, z, ckv_off, g, wk, wv, wkr, ct, st, transpose_v):
    m, d = hb.shape
    r = g.shape[0]
    heads = MLA_HEADS
    tm = _tile(math.gcd(m, ct.shape[0]), 512)
    npos = ct.shape[0] // tm
    assert ckv_off % r == 0
    wk_out = heads * (MLA_NOPE_DIM + V7X_LANES)
    wv_out = heads * MLA_V_DIM
    if transpose_v:
        v_spec = pl.BlockSpec((wv_out, tm), lambda i: (0, i))
        v_shape = jax.ShapeDtypeStruct((wv_out, m), BF16)
    else:
        v_spec = pl.BlockSpec((tm, wv_out), lambda i: (i, 0))
        v_shape = jax.ShapeDtypeStruct((m, wv_out), BF16)
    vm = (2 * (_nbytes((tm, d), BF16) + _nbytes((tm, r), BF16) + _nbytes(wk.shape, BF16) + _nbytes(wv.shape, BF16)
               + _nbytes(wkr.shape, BF16) + 2 * tm * V7X_LANES * 4 + _nbytes((tm, wk_out), BF16)
               + _nbytes((tm, wv_out), BF16)) + 2 * _nbytes((tm, wv_out), F32) + _nbytes((tm, r), F32) * 2)
    tab = pl.BlockSpec((tm, V7X_LANES), lambda i: (i % npos, 0))
    return pl.pallas_call(
        functools.partial(_kv_proj_kernel, heads=heads, transpose_v=transpose_v),
        grid=(m // tm,),
        in_specs=[pl.BlockSpec((tm, d), lambda i: (i, 0)),
                  pl.BlockSpec((tm, r), lambda i: (i, ckv_off // r)),
                  pl.BlockSpec((1, r), lambda i: (0, 0)),
                  pl.BlockSpec(wk.shape, lambda i: (0, 0)),
                  pl.BlockSpec(wv.shape, lambda i: (0, 0)),
                  pl.BlockSpec(wkr.shape, lambda i: (0, 0)),
                  tab, tab],
        out_specs=[pl.BlockSpec((tm, wk_out), lambda i: (i, 0)), v_spec,
                   pl.BlockSpec((tm, V7X_LANES), lambda i: (i, 0))],
        out_shape=[jax.ShapeDtypeStruct((m, wk_out), BF16), v_shape, jax.ShapeDtypeStruct((m, V7X_LANES), F32)],
        compiler_params=_cparams(("parallel",), vm),
        name="mla_kv_proj",
    )(hb, z, g.reshape(1, r), wk, wv, wkr, ct, st)


def _with_ones(v):
    return jnp.concatenate([v, jnp.ones(v.shape, v.dtype)], axis=1)


ONES_ROWS = 16


UNDERFLOW_GUARD = 2.0 ** -100


def _mla_attn_kernel(q_ref, k_ref, km_ref, vt_ref, vm_ref, o_ref, m_ref, acc_ref, *, tk, n_meta, hb):
    wq = MLA_NOPE_DIM + V7X_LANES
    dv = MLA_V_DIM
    tq = acc_ref.shape[2]
    n_chunks = k_ref.shape[0] // tk
    pad = V7X_LANES - n_meta

    def q_of(j, shifted):
        q = q_ref[:, j * wq:(j + 1) * wq]
        if not shifted:
            lane = lax.broadcasted_iota(jnp.int32, q.shape, 1)
            q = jnp.where(lane == MLA_NOPE_DIM + SHIFT_LANE, 0.0, q).astype(BF16)
        if q.shape[0] < tq:
            q = jnp.concatenate([q, jnp.zeros((tq - q.shape[0], wq), BF16)], axis=0)
        return q

    def scores(j, c, shifted):
        start = pl.multiple_of(c * tk, tk)
        return _dot_t(k_ref[pl.ds(start, tk), j * wq:(j + 1) * wq], q_of(j, shifted))

    def vt_of(j, c):
        start = pl.multiple_of(c * tk, tk)
        return jnp.concatenate([vt_ref[j * dv:(j + 1) * dv, pl.ds(start, tk)], jnp.ones((ONES_ROWS, tk), BF16)], axis=0)

    def meta_scores(j, shifted):
        kmp = jnp.concatenate([km_ref[:, j * wq:(j + 1) * wq], jnp.zeros((pad, wq), BF16)], axis=0)
        s = _dot_t(kmp, q_of(j, shifted))
        row = lax.broadcasted_iota(jnp.int32, s.shape, 0)
        return jnp.where(row < n_meta, s, NEG)

    def meta_vt(j):
        vmp = jnp.concatenate([vm_ref[:, j * dv:(j + 1) * dv].astype(F32), jnp.zeros((pad, dv), F32)], axis=0)
        return jnp.concatenate([vmp.T.astype(BF16), jnp.ones((ONES_ROWS, V7X_LANES), BF16)], axis=0)

    def store(j, acc):
        o = (acc[:dv] / acc[dv:dv + 1]).T
        o_ref[:, j * dv:(j + 1) * dv] = o[:o_ref.shape[0]].astype(o_ref.dtype)

    acc = [_dot(meta_vt(j), jnp.exp2(meta_scores(j, True)).astype(BF16)) for j in range(hb)]
    s = [scores(j, 0, True) for j in range(hb)]
    for c in range(n_chunks):
        for j in range(hb):
            p = jnp.exp2(s[j]).astype(BF16)
            if c + 1 < n_chunks:
                s[j] = scores(j, c + 1, True)
            acc[j] = acc[j] + _dot(vt_of(j, c), p)
    for j in range(hb):
        store(j, acc[j])

    def online_head(j):
        def accumulate(c, sc):
            m = m_ref[j, 0:1]
            m_new = jnp.maximum(m, jnp.max(sc, axis=0, keepdims=True))
            acc_ref[j] = jnp.exp2(m - m_new) * acc_ref[j] + _dot(vt_of(j, c), jnp.exp2(sc - m_new).astype(BF16))
            m_ref[j, 0:1] = m_new

        sm = meta_scores(j, False)
        m0 = jnp.max(sm, axis=0, keepdims=True)
        m_ref[j, 0:1] = m0
        acc_ref[j] = _dot(meta_vt(j), jnp.exp2(sm - m0).astype(BF16))

        def body(c, carry):
            accumulate(c, scores(j, c, False))
            return carry

        lax.fori_loop(0, n_chunks, body, 0)
        store(j, acc_ref[j])

    for j in range(hb):
        denom_ok = jnp.min(acc[j][dv:dv + 1]) >= UNDERFLOW_GUARD

        @pl.when(jnp.logical_not(denom_ok))
        def _():
            online_head(j)


def _mla_attn(q, k, km, vt, vmeta, batch, n_meta):
    heads = MLA_HEADS
    hb = 2
    wq = MLA_NOPE_DIM + V7X_LANES
    dv = MLA_V_DIM
    rows_q = q.shape[0] // batch
    seq = k.shape[0] // batch
    tq = _tile(rows_q, 1024)
    tk = _tile(seq, 4 * V7X_MXU_DIM)
    nq = rows_q // tq
    tqp = max(tq, V7X_LANES)
    vm = (2 * hb * (_nbytes((tq, wq), BF16) + _nbytes((seq, wq), BF16) + _nbytes((seq, dv), BF16)
                    + _nbytes((tq, dv), BF16)) + hb * (6 * _nbytes((tk, tqp), F32) + 2 * _nbytes((2 * dv, tqp), F32)))
    return pl.pallas_call(
        functools.partial(_mla_attn_kernel, tk=tk, n_meta=n_meta, hb=hb),
        grid=(batch, heads // hb, nq),
        in_specs=[pl.BlockSpec((tq, hb * wq), lambda b, h, i: (b * nq + i, h)),
                  pl.BlockSpec((seq, hb * wq), lambda b, h, i: (b, h)),
                  pl.BlockSpec((n_meta, hb * wq), lambda b, h, i: (b, h)),
                  pl.BlockSpec((hb * dv, seq), lambda b, h, i: (h, b)),
                  pl.BlockSpec((n_meta, hb * dv), lambda b, h, i: (b, h))],
        out_specs=pl.BlockSpec((tq, hb * dv), lambda b, h, i: (b * nq + i, h)),
        out_shape=jax.ShapeDtypeStruct((q.shape[0], heads * dv), BF16),
        scratch_shapes=[pltpu.VMEM((hb, 8, tqp), F32), pltpu.VMEM((hb, dv + ONES_ROWS, tqp), F32)],
        compiler_params=_cparams(("parallel", "parallel", "parallel"), vm),
        name="mla_attn",
    )(q, k, km, vt, vmeta)


VIS_SINK = 2


def _bias_kernel(rb_ref, sink_ref, bk_ref, vis_ref, o_ref, *, n_buckets):
    h = pl.program_id(1)
    bk = bk_ref[0]
    vis = vis_ref[0]
    acc = jnp.zeros(bk.shape, F32)
    for j in range(n_buckets):
        acc = jnp.where(bk == j, rb_ref[j, h] * LOG2E, acc)
    o_ref[0, 0] = jnp.where(vis == VIS_SINK, sink_ref[h] * LOG2E, jnp.where(vis != 0, acc, NEG))


def _bias_tables(rel_bias, sink, bucket, vis):
    t, r, s = bucket.shape
    heads = rel_bias.shape[1]
    blk = pl.BlockSpec((1, r, s), lambda i, h: (i, 0, 0))
    smem = pl.BlockSpec(memory_space=pltpu.SMEM)
    return pl.pallas_call(
        functools.partial(_bias_kernel, n_buckets=rel_bias.shape[0]),
        grid=(t, heads),
        in_specs=[smem, smem, blk, blk],
        out_specs=pl.BlockSpec((1, 1, r, s), lambda i, h: (i, h, 0, 0)),
        out_shape=jax.ShapeDtypeStruct((t, heads, r, s), F32),
        compiler_params=_cparams(("parallel", "parallel"), 16 * r * s * 4),
        name="swa_bias_table",
    )(rel_bias, sink, bucket, vis)


def _swa_kernel(q_ref, kp_ref, ko_ref, kn_ref, km_ref, vp_ref, vo_ref, vn_ref, vm_ref, bias_ref, o_ref,
                *, kv_heads, group, n_meta):
    d = SWA_HEAD_DIM
    r = q_ref.shape[0]
    zpad = jnp.zeros((BLOCK - n_meta, d), BF16)

    for g in range(kv_heads):
        cols = slice(g * d, (g + 1) * d)
        kcat = jnp.concatenate([kp_ref[:, cols], ko_ref[:, cols], kn_ref[:, cols], km_ref[:, cols], zpad], axis=0)
        vcat = jnp.concatenate([vp_ref[:, cols], vo_ref[:, cols], vn_ref[:, cols], vm_ref[:, cols], zpad], axis=0)
        heads = range(g * group, (g + 1) * group)
        q = jnp.concatenate([q_ref[:, h * d:(h + 1) * d] for h in heads], axis=0)
        bias = jnp.concatenate([bias_ref[0, h] for h in heads], axis=0)
        s = _dot_t(q, kcat) + bias
        m = jnp.max(s, axis=-1, keepdims=True)
        acc = _dot(jnp.exp2(s - m).astype(BF16), _with_ones(vcat))
        o = acc[:, :d] / acc[:, d:]
        for j, h in enumerate(heads):
            o_ref[:, h * d:(h + 1) * d] = o[j * r:(j + 1) * r].astype(o_ref.dtype)


def _swa_attn(q_src, q_rows, kv_src, kv_meta, k_off, v_off, bias, type_of_block, kv_block_of, batch, n_grid_blk,
              n_meta):
    d = SWA_HEAD_DIM
    group = SWA_HEADS // SWA_KV_HEADS
    wq, wkv = SWA_HEADS * d, SWA_KV_HEADS * d
    nblk_kv = kv_src.shape[0] // batch // BLOCK
    s_keys = bias.shape[-1]
    assert k_off % wkv == 0 and v_off % wkv == 0
    kc, vc = k_off // wkv, v_off // wkv

    def kv_spec(which, col):
        return pl.BlockSpec((BLOCK, wkv), lambda b, i: (b * nblk_kv + kv_block_of(i)[which], col))

    def meta_spec(col):
        return pl.BlockSpec((n_meta, wkv), lambda b, i: (b, col))

    rows = group * q_rows
    vm = (2 * (2 * _nbytes((q_rows, wq), BF16) + 8 * _nbytes((BLOCK, wkv), BF16)
               + _nbytes((SWA_HEADS, q_rows, s_keys), F32))
          + SWA_KV_HEADS * (4 * _nbytes((rows, s_keys), F32) + _nbytes((rows, 2 * d), F32)))
    return pl.pallas_call(
        functools.partial(_swa_kernel, kv_heads=SWA_KV_HEADS, group=group, n_meta=n_meta),
        grid=(batch, n_grid_blk),
        in_specs=[pl.BlockSpec((q_rows, wq), lambda b, i: (b * n_grid_blk + i, 0)),
                  kv_spec(0, kc), kv_spec(1, kc), kv_spec(2, kc), meta_spec(kc),
                  kv_spec(0, vc), kv_spec(1, vc), kv_spec(2, vc), meta_spec(vc),
                  pl.BlockSpec((1, SWA_HEADS, q_rows, s_keys), lambda b, i: (type_of_block(i), 0, 0, 0))],
        out_specs=pl.BlockSpec((q_rows, wq), lambda b, i: (b * n_grid_blk + i, 0)),
        out_shape=jax.ShapeDtypeStruct((batch * n_grid_blk * q_rows, wq), BF16),
        compiler_params=_cparams(("parallel", "parallel"), vm),
        name="swa_attn",
    )(q_src, kv_src, kv_src, kv_src, kv_meta, kv_src, kv_src, kv_src, kv_meta, bias)


def _merge_out_ln_kernel(oa_ref, ob_ref, ga_ref, gb_ref, wa_ref, wb_ref, wo_ref, h_ref, g_ref, b_ref, ho_ref, *, alpha):
    pa = _dot(oa_ref[...], wa_ref[...])
    pb = _dot(ob_ref[...], wb_ref[...])
    ga = jax.nn.sigmoid(ga_ref[...].astype(F32))
    gb = jax.nn.sigmoid(gb_ref[...].astype(F32))
    merged = (ga * pa + gb * pb).astype(BF16)
    y = alpha * h_ref[...] + _dot(merged, wo_ref[...])
    ho_ref[...] = _ln_rows(y, g_ref[...], b_ref[...])


def _merge_out_ln(oa, ob, z, ga_off, gb_off, wa, wb, wo, h, g, b, alpha):
    m, d = h.shape
    ka, kb = oa.shape[1], ob.shape[1]
    tm = _tile(m, 256)
    assert ga_off % d == 0 and gb_off % d == 0
    resident = lambda w: pl.BlockSpec(w.shape, lambda i: (0, 0), pipeline_mode=pl.Buffered(1))
    row = lambda width, col: pl.BlockSpec((tm, width), lambda i: (i, col))
    vec = pl.BlockSpec((1, d), lambda i: (0, 0))
    vm = (_nbytes(wa.shape, BF16) + _nbytes(wb.shape, BF16) + _nbytes(wo.shape, BF16)
          + 2 * (_nbytes((tm, ka + kb + 2 * d), BF16) + 2 * _nbytes((tm, d), F32)) + 5 * _nbytes((tm, d), F32))
    return pl.pallas_call(
        functools.partial(_merge_out_ln_kernel, alpha=alpha),
        grid=(m // tm,),
        in_specs=[row(ka, 0), row(kb, 0), row(d, ga_off // d), row(d, gb_off // d),
                  resident(wa), resident(wb), resident(wo), row(d, 0), vec, vec],
        out_specs=row(d, 0),
        out_shape=jax.ShapeDtypeStruct((m, d), F32),
        compiler_params=_cparams(("parallel",), vm),
        name="merge_out_ln",
    )(oa, ob, z, z, wa, wb, wo, h, g.reshape(1, d), b.reshape(1, d))


def _swiglu(a1, a3):
    return a1 * jax.nn.sigmoid(a1) * a3


def _ffn_ln_kernel(h_ref, w1_ref, w3_ref, w2_ref, lg_ref, lb_ref, ho_ref, hbo_ref, xb_ref, acc_ref, *, alpha):
    j = pl.program_id(1)

    @pl.when(j == 0)
    def _():
        xb_ref[...] = h_ref[...].astype(BF16)
        acc_ref[...] = jnp.zeros_like(acc_ref)

    xb = xb_ref[...]
    gact = _swiglu(_dot(xb, w1_ref[...]), _dot(xb, w3_ref[...])).astype(BF16)
    acc_ref[...] += _dot(gact, w2_ref[...])

    @pl.when(j == pl.num_programs(1) - 1)
    def _():
        y = _ln_rows(alpha * h_ref[...] + acc_ref[...], lg_ref[...], lb_ref[...])
        ho_ref[...] = y
        hbo_ref[...] = y.astype(BF16)


def _ffn_ln(h, w1, w3, w2, lg, lb, alpha):
    m, d = h.shape
    f = w1.shape[1]
    tm = _tile(m, 512)
    tf = _tile(f, 512)
    vm = (2 * (2 * _nbytes((tm, d), F32) + 3 * _nbytes((d, tf), BF16) + _nbytes((tm, d), BF16))
          + _nbytes((tm, d), BF16) + _nbytes((tm, d), F32) + 4 * _nbytes((tm, tf), F32) + 2 * _nbytes((tm, d), F32))
    row = pl.BlockSpec((tm, d), lambda i, j: (i, 0))
    vec = pl.BlockSpec((1, d), lambda i, j: (0, 0))
    wup = pl.BlockSpec((d, tf), lambda i, j: (0, j))
    return pl.pallas_call(
        functools.partial(_ffn_ln_kernel, alpha=alpha),
        grid=(m // tm, f // tf),
        in_specs=[row, wup, wup, pl.BlockSpec((tf, d), lambda i, j: (j, 0)), vec, vec],
        out_specs=[row, row],
        out_shape=[jax.ShapeDtypeStruct((m, d), F32), jax.ShapeDtypeStruct((m, d), BF16)],
        scratch_shapes=[pltpu.VMEM((tm, d), BF16), pltpu.VMEM((tm, d), F32)],
        compiler_params=_cparams(("parallel", "arbitrary"), vm),
        name="ffn_ln",
    )(h, w1, w3, w2, lg.reshape(1, d), lb.reshape(1, d))


_PK_E1, _PK_E2, _PK_G1, _PK_G2, _PK_R1, _PK_R2 = range(6)


def _router_kernel(h_ref, w_ref, b_ref, pk_ref, cnt_ref, carry_ref, *, n_exp):
    @pl.when(pl.program_id(0) == 0)
    def _():
        carry_ref[...] = jnp.zeros_like(carry_ref)

    logits = jnp.dot(h_ref[...], w_ref[...], preferred_element_type=F32, precision=lax.Precision.HIGHEST) + b_ref[...]
    tm, nl = logits.shape
    lane = lax.broadcasted_iota(jnp.int32, (tm, nl), 1).astype(F32)
    s = jnp.where(lane < n_exp, logits, -jnp.inf)
    m1 = jnp.max(s, axis=-1, keepdims=True)
    i1 = jnp.min(jnp.where(s == m1, lane, float(nl)), axis=-1, keepdims=True)
    s2 = jnp.where(lane == i1, -jnp.inf, s)
    m2 = jnp.max(s2, axis=-1, keepdims=True)
    i2 = jnp.min(jnp.where(s2 == m2, lane, float(nl)), axis=-1, keepdims=True)
    e = jnp.exp(m2 - m1)
    g1 = 1.0 / (1.0 + e)
    g2 = e / (1.0 + e)
    oh1 = lane == i1
    oh2 = lane == i2
    both = jnp.where(oh1 | oh2, 1.0, 0.0)
    r = lax.broadcasted_iota(jnp.int32, (tm, tm), 0)
    c = lax.broadcasted_iota(jnp.int32, (tm, tm), 1)
    tri = jnp.where(c < r, 1.0, 0.0).astype(BF16)
    prefix = _dot(tri, both.astype(BF16)) + carry_ref[...]
    r1 = jnp.sum(jnp.where(oh1, prefix, 0.0), axis=-1, keepdims=True)
    r2 = jnp.sum(jnp.where(oh2, prefix, 0.0), axis=-1, keepdims=True)
    carry_ref[...] += jnp.sum(both, axis=0, keepdims=True)
    cnt_ref[...] = carry_ref[...]
    pk = jnp.zeros((tm, nl), F32)
    for col, val in ((_PK_E1, i1), (_PK_E2, i2), (_PK_G1, g1), (_PK_G2, g2), (_PK_R1, r1), (_PK_R2, r2)):
        pk = jnp.where(lane == col, val, pk)
    pk_ref[...] = pk


def _router(h, w_pad, b_pad, n_exp):
    m, d = h.shape
    nl = w_pad.shape[1]
    tm = _tile(m, 512)
    vm = 2 * (_nbytes((tm, d), F32) + _nbytes((d, nl), F32) + _nbytes((tm, nl), F32)) + _nbytes((tm, tm), F32) * 3
    return pl.pallas_call(
        functools.partial(_router_kernel, n_exp=n_exp),
        grid=(m // tm,),
        in_specs=[pl.BlockSpec((tm, d), lambda i: (i, 0)), pl.BlockSpec((d, nl), lambda i: (0, 0)),
                  pl.BlockSpec((1, nl), lambda i: (0, 0))],
        out_specs=[pl.BlockSpec((tm, nl), lambda i: (i, 0)), pl.BlockSpec((1, nl), lambda i: (0, 0))],
        out_shape=[jax.ShapeDtypeStruct((m, nl), F32), jax.ShapeDtypeStruct((1, nl), F32)],
        scratch_shapes=[pltpu.VMEM((1, nl), F32)],
        compiler_params=_cparams(("arbitrary",), vm),
        name="moe_router",
    )(h, w_pad, b_pad)


def _row_copy(src_ref, src_row, dst_ref, dst_row, sem):
    return pltpu.make_async_copy(src_ref.at[pl.ds(src_row, 1)], dst_ref.at[pl.ds(dst_row, 1)], sem)


DMA_ISSUE_UNROLL = 8


def _dispatch_kernel(d1_ref, d2_ref, pad_ref, h_ref, zero_ref, xs_ref, sem, *, n_ranges):
    tb = h_ref.shape[0]

    def issue(t, carry):
        _row_copy(h_ref, t, xs_ref, d1_ref[0, 0, t], sem).start()
        _row_copy(h_ref, t, xs_ref, d2_ref[0, 0, t], sem).start()
        return carry

    lax.fori_loop(0, tb, issue, 0, unroll=DMA_ISSUE_UNROLL)

    @pl.when(pl.program_id(0) == 0)
    def _():
        for e in range(n_ranges):
            base = pad_ref[0, e]
            n = pad_ref[1, e]
            lax.fori_loop(0, n, lambda j, c: (_row_copy(zero_ref, 0, xs_ref, base + j, sem).start(), c)[1], 0)
            lax.fori_loop(0, n, lambda j, c: (_row_copy(zero_ref, 0, xs_ref, 0, sem).wait(), c)[1], 0)

    for _ in range(TOP_K):
        pltpu.make_async_copy(h_ref, xs_ref.at[pl.ds(0, tb)], sem).wait()


def _dispatch(h, d1, d2, pad_info, n_slots):
    m, d = h.shape
    tb = _tile(m, 512)
    nt = m // tb
    idx = pl.BlockSpec((1, 1, tb), lambda i: (i, 0, 0), memory_space=pltpu.SMEM)
    zero = jnp.zeros((8, d), F32)
    return pl.pallas_call(
        functools.partial(_dispatch_kernel, n_ranges=pad_info.shape[1]),
        grid=(nt,),
        in_specs=[idx, idx, pl.BlockSpec(memory_space=pltpu.SMEM), pl.BlockSpec((tb, d), lambda i: (i, 0)),
                  pl.BlockSpec((8, d), lambda i: (0, 0))],
        out_specs=pl.BlockSpec(memory_space=pl.ANY),
        out_shape=jax.ShapeDtypeStruct((n_slots, d), F32),
        scratch_shapes=[pltpu.SemaphoreType.DMA(())],
        compiler_params=_cparams(("arbitrary",), 4 * tb * d * 4),
        name="moe_dispatch",
    )(d1.reshape(nt, 1, tb), d2.reshape(nt, 1, tb), pad_info, h, zero)


def _moe_ffn_kernel(te_ref, ts_ref, tv_ref, x_ref, w1_ref, w3_ref, w2_ref, y_ref, xb_ref):
    i = pl.program_id(0)
    j = pl.program_id(1)

    @pl.when(tv_ref[i] != 0)
    def _():
        @pl.when(j == 0)
        def _():
            xb_ref[...] = x_ref[...].astype(BF16)
            y_ref[...] = jnp.zeros_like(y_ref)

        xb = xb_ref[...]
        gact = _swiglu(_dot(xb, w1_ref[0]), _dot(xb, w3_ref[0])).astype(BF16)
        y_ref[...] += _dot(gact, w2_ref[0])

    @pl.when((tv_ref[i] == 0) & (j == 0))
    def _():
        y_ref[...] = jnp.zeros_like(y_ref)


def _moe_ffn(xs, w1, w3, w2, tile_expert, tile_src, tile_valid, tm):
    n_slots, d = xs.shape
    f = w1.shape[2]
    tf = _tile(f, 512)
    nf = f // tf
    n_tiles = n_slots // tm

    def jj(i, j, tv):
        return jnp.where(tv[i] != 0, j, nf - 1)

    vm = (2 * (2 * _nbytes((tm, d), F32) + 3 * _nbytes((d, tf), BF16)) + _nbytes((tm, d), BF16)
          + 4 * _nbytes((tm, tf), F32) + _nbytes((tm, d), F32))
    grid_spec = pltpu.PrefetchScalarGridSpec(
        num_scalar_prefetch=3,
        grid=(n_tiles, nf),
        in_specs=[pl.BlockSpec((tm, d), lambda i, j, te, ts, tv: (ts[i], 0)),
                  pl.BlockSpec((1, d, tf), lambda i, j, te, ts, tv: (te[i], 0, jj(i, j, tv))),
                  pl.BlockSpec((1, d, tf), lambda i, j, te, ts, tv: (te[i], 0, jj(i, j, tv))),
                  pl.BlockSpec((1, tf, d), lambda i, j, te, ts, tv: (te[i], jj(i, j, tv), 0))],
        out_specs=pl.BlockSpec((tm, d), lambda i, j, te, ts, tv: (i, 0)),
        scratch_shapes=[pltpu.VMEM((tm, d), BF16)],
    )
    return pl.pallas_call(
        _moe_ffn_kernel,
        grid_spec=grid_spec,
        out_shape=jax.ShapeDtypeStruct((n_slots, d), F32),
        compiler_params=_cparams(("arbitrary", "arbitrary"), vm),
        name="moe_expert_ffn",
    )(tile_expert, tile_src, tile_valid, xs, w1, w3, w2)


def _combine_ln_kernel(d1_ref, d2_ref, d1n_ref, d2n_ref, pk_ref, h_ref, y_ref, lg_ref, lb_ref, o_ref, ybuf_ref, sems,
                       *, alpha, tb):
    step = pl.program_id(0)

    def gather(i1_ref, i2_ref, off, buf):
        def issue(t, carry):
            _row_copy(y_ref, i1_ref[0, 0, off + t], ybuf_ref.at[buf, 0], t, sems.at[buf]).start()
            _row_copy(y_ref, i2_ref[0, 0, off + t], ybuf_ref.at[buf, 1], t, sems.at[buf]).start()
            return carry

        lax.fori_loop(0, tb, issue, 0, unroll=DMA_ISSUE_UNROLL)

    def wait(buf):
        for k in range(TOP_K):
            pltpu.make_async_copy(y_ref.at[pl.ds(0, tb)], ybuf_ref.at[buf, k], sems.at[buf]).wait()

    def finish(buf, rows):
        pk = pk_ref[rows, :]
        f = pk[:, _PK_G1:_PK_G1 + 1] * ybuf_ref[buf, 0] + pk[:, _PK_G2:_PK_G2 + 1] * ybuf_ref[buf, 1]
        o_ref[rows, :] = _ln_rows(alpha * h_ref[rows, :] + f, lg_ref[...], lb_ref[...])

    @pl.when(step == 0)
    def _():
        gather(d1_ref, d2_ref, 0, 0)

    gather(d1_ref, d2_ref, tb, 1)
    wait(0)
    finish(0, slice(0, tb))

    @pl.when(step < pl.num_programs(0) - 1)
    def _():
        gather(d1n_ref, d2n_ref, 0, 0)

    wait(1)
    finish(1, slice(tb, 2 * tb))


def _combine_ln(h, pk, ys, d1, d2, lg, lb, alpha):
    m, d = h.shape
    tb = _tile(m // 2, 256)
    ns = m // (2 * tb)
    idx = pl.BlockSpec((1, 1, 2 * tb), lambda i: (i, 0, 0), memory_space=pltpu.SMEM)
    idx_next = pl.BlockSpec((1, 1, 2 * tb), lambda i: (jnp.minimum(i + 1, ns - 1), 0, 0), memory_space=pltpu.SMEM)
    row = pl.BlockSpec((2 * tb, d), lambda i: (i, 0))
    vec = pl.BlockSpec((1, d), lambda i: (0, 0))
    vm = (2 * (2 * _nbytes((2 * tb, d), F32) + _nbytes((2 * tb, pk.shape[1]), F32)) + 4 * _nbytes((tb, d), F32)
          + 4 * _nbytes((tb, d), F32))
    d1b, d2b = d1.reshape(ns, 1, 2 * tb), d2.reshape(ns, 1, 2 * tb)
    return pl.pallas_call(
        functools.partial(_combine_ln_kernel, alpha=alpha, tb=tb),
        grid=(ns,),
        in_specs=[idx, idx, idx_next, idx_next, pl.BlockSpec((2 * tb, pk.shape[1]), lambda i: (i, 0)), row,
                  pl.BlockSpec(memory_space=pl.ANY), vec, vec],
        out_specs=row,
        out_shape=jax.ShapeDtypeStruct((m, d), F32),
        scratch_shapes=[pltpu.VMEM((2, TOP_K, tb, d), F32), pltpu.SemaphoreType.DMA((2,))],
        compiler_params=_cparams(("arbitrary",), vm),
        name="moe_combine_ln",
    )(d1b, d2b, d1b, d2b, pk, h, ys, lg.reshape(1, d), lb.reshape(1, d))


def _rope_tables(n_tok):
    pos = jnp.arange(n_tok, dtype=F32)
    inv = ROPE_THETA ** (-jnp.arange(0, MLA_ROPE_DIM, 2, dtype=F32) / MLA_ROPE_DIM)
    ang = pos[:, None] * inv[None, :]
    cos, sin = jnp.cos(ang), jnp.sin(ang)
    zero = jnp.zeros_like(cos)
    return jnp.concatenate([cos, cos, zero, zero], -1), jnp.concatenate([-sin, sin, zero, zero], -1)


def _rel_bucket(rel):
    nb = N_BUCKETS // 2
    max_exact = nb // 2
    n = jnp.abs(rel)
    large = max_exact + (jnp.log(jnp.maximum(n, 1).astype(F32) / max_exact)
                         / math.log(MAX_DISTANCE / max_exact) * (nb - max_exact)).astype(jnp.int32)
    large = jnp.minimum(large, nb - 1)
    return jnp.where(rel > 0, nb, 0) + jnp.where(n < max_exact, n, large)


def _swa_index_tables(n_meta, n_real):
    nblk = n_real // BLOCK
    band = jnp.arange(3 * BLOCK)
    meta_pos = jnp.arange(n_meta)
    pad = BLOCK - n_meta

    def one(start, q_pos, first_block_only=False):
        r_key = start - BLOCK + band
        in_range = (r_key >= 0) & (r_key < n_real)
        if first_block_only:
            in_range = in_range & (r_key < BLOCK)
        k_pos = jnp.concatenate([n_meta + r_key, meta_pos, jnp.zeros((pad,), jnp.int32)])
        always = jnp.concatenate([jnp.zeros((3 * BLOCK,), bool), jnp.ones((n_meta,), bool), jnp.zeros((pad,), bool)])
        live = jnp.concatenate([in_range, jnp.ones((n_meta,), bool), jnp.zeros((pad,), bool)])
        rel = k_pos[None, :] - q_pos[:, None]
        vis = always[None, :] | (live[None, :] & (jnp.abs(rel) <= WINDOW))
        return _rel_bucket(rel), vis

    q_local = jnp.arange(BLOCK)
    tabs = [one(0, n_meta + q_local),
            one(BLOCK * min(1, nblk - 1), n_meta + BLOCK * min(1, nblk - 1) + q_local),
            one(BLOCK * (nblk - 1), n_meta + BLOCK * (nblk - 1) + q_local),
            one(0, jnp.where(q_local < n_meta, q_local, 0), first_block_only=True)]
    bucket = jnp.stack([t[0] for t in tabs]).astype(jnp.int32)
    vis = jnp.stack([t[1] for t in tabs]).astype(jnp.int32)
    vis = vis.at[:, :, 3 * BLOCK + n_meta].set(VIS_SINK)
    return bucket, vis


def _layer_weights(l, w_in, w_uq, w_ukv, w_proj_a, w_proj_b, w_out, d_model, q_rank, kv_rank):
    hd = SWA_HEADS * SWA_HEAD_DIM
    kvd = SWA_KV_HEADS * SWA_HEAD_DIM
    splits = (q_rank, kv_rank, MLA_ROPE_DIM, hd, kvd, kvd, d_model, d_model)
    off = [0]
    for s in splits:
        off.append(off[-1] + s)
    swa_q_scale = SWA_HEAD_DIM ** -0.5 * LOG2E
    col = lambda i: w_in[l][:, off[i]:off[i + 1]] * (swa_q_scale if i == 3 else 1.0)
    order = (3, 6, 7, 1, 4, 5, 0)
    w_z = jnp.concatenate([col(i) for i in order], axis=1).astype(BF16)
    z_off = {}
    o = 0
    for i in order:
        z_off[i] = o
        o += splits[i]
    half = MLA_ROPE_DIM // 2
    kr = col(2)
    w_kr = jnp.concatenate([kr[:, :half], kr[:, half:], kr[:, :half], kr[:, half:]], axis=1).astype(BF16)
    uq = w_uq[l].reshape(q_rank, MLA_HEADS, MLA_NOPE_DIM + MLA_ROPE_DIM)
    uq_n = uq[:, :, :MLA_NOPE_DIM].reshape(q_rank, MLA_HEADS * MLA_NOPE_DIM).astype(BF16)
    r1 = uq[:, :, MLA_NOPE_DIM:MLA_NOPE_DIM + half]
    r2 = uq[:, :, MLA_NOPE_DIM + half:]
    uq_r = jnp.concatenate([r1, r2, r1, r2], axis=2).reshape(q_rank, MLA_HEADS * V7X_LANES).astype(BF16)
    ukv = w_ukv[l].reshape(kv_rank, MLA_HEADS, MLA_NOPE_DIM + MLA_V_DIM)
    uk = ukv[:, :, :MLA_NOPE_DIM].reshape(kv_rank, MLA_HEADS * MLA_NOPE_DIM).astype(BF16)
    uv = ukv[:, :, MLA_NOPE_DIM:].reshape(kv_rank, MLA_HEADS * MLA_V_DIM).astype(BF16)
    return dict(w_z=w_z, z_off=z_off, w_kr=w_kr, uq_n=uq_n, uq_r=uq_r, uk=uk, uv=uv, uv_t=uv.T,
                wa=w_proj_a[l].astype(BF16), wb=w_proj_b[l].astype(BF16), wo=w_out[l].astype(BF16))


def kernel(x, meta_tokens, emb_ln_g, emb_ln_b, rel_bias, w_in, q_norm_g, kv_norm_g, w_uq, w_ukv, sink_logits,
           w_proj_a, w_proj_b, w_out, ln_mix_g, ln_mix_b, ln_ffn_g, ln_ffn_b, ffn_w1, ffn_w3, ffn_w2, router_w,
           router_b, moe_w1, moe_w3, moe_w2):
    bsz, seq, d = x.shape
    n_meta = meta_tokens.shape[0]
    depth = w_in.shape[0]
    q_rank = q_norm_g.shape[1]
    kv_rank = kv_norm_g.shape[1]
    alpha = (2 * depth) ** 0.25
    nblk = seq // BLOCK
    assert seq % BLOCK == 0 and n_meta % 16 == 0 and n_meta < BLOCK

    hr, hr_b = _embed_ln(x.reshape(bsz * seq, d), emb_ln_g, emb_ln_b)
    hm1, hm1_b = _embed_ln(meta_tokens.astype(x.dtype), emb_ln_g, emb_ln_b)
    hm, hm_b = jnp.tile(hm1, (bsz, 1)), jnp.tile(hm1_b, (bsz, 1))

    ct, st = _rope_tables(n_meta + seq)
    ct_r, st_r = ct[n_meta:], st[n_meta:]
    ct_m, st_m = jnp.tile(ct[:n_meta], (bsz, 1)), jnp.tile(st[:n_meta], (bsz, 1))
    bucket, vis = _swa_index_tables(n_meta, seq)

    last_blk = nblk - 1
    type_real = lambda i: jnp.where(i == 0, 0, jnp.where(i == last_blk, 2, 1))
    kv_real = lambda i: (jnp.maximum(i - 1, 0), i, jnp.minimum(i + 1, last_blk))
    type_meta = lambda i: 0
    kv_meta_q = lambda i: (0, 0, 0)

    for l in range(depth):
        last = l == depth - 1
        lw = _layer_weights(l, w_in, w_uq, w_ukv, w_proj_a, w_proj_b, w_out, d, q_rank, kv_rank)
        zo = lw["z_off"]
        bias_all = _bias_tables(rel_bias, sink_logits[l], bucket, vis)
        bias_real = bias_all[:3]
        bias_meta = bias_all[3:, :, :n_meta]

        z_r = _matmul(hr_b, lw["w_z"], BF16)
        z_m = _matmul(hm_b, lw["w_z"], BF16)
        k_r, v_r, ksq_r = _kv_proj(hr_b, z_r, zo[1], kv_norm_g[l], lw["uk"], lw["uv_t"], lw["w_kr"], ct_r, st_r, True)
        k_m, v_m, ksq_m = _kv_proj(hm_b, z_m, zo[1], kv_norm_g[l], lw["uk"], lw["uv"], lw["w_kr"], ct_m, st_m, False)
        kmax = jnp.maximum(ksq_r.reshape(bsz, seq, -1).max(axis=1), ksq_m.reshape(bsz, n_meta, -1).max(axis=1))
        kmax_r, kmax_m = jnp.repeat(kmax, seq, axis=0), jnp.repeat(kmax, n_meta, axis=0)
        q_r = _q_proj(z_r, zo[0], q_norm_g[l], lw["uq_n"], lw["uq_r"], ct_r, st_r, kmax_r)
        oa_r = _mla_attn(q_r, k_r, k_m, v_r, v_m, bsz, n_meta)
        ob_r = _swa_attn(z_r, BLOCK, z_r, z_m, zo[4], zo[5], bias_real, type_real, kv_real, bsz, nblk, n_meta)
        hr = _merge_out_ln(oa_r, ob_r, z_r, zo[6], zo[7], lw["wa"], lw["wb"], lw["wo"], hr, ln_mix_g[l], ln_mix_b[l],
                           alpha)
        if not last:
            q_m = _q_proj(z_m, zo[0], q_norm_g[l], lw["uq_n"], lw["uq_r"], ct_m, st_m, kmax_m)
            oa_m = _mla_attn(q_m, k_r, k_m, v_r, v_m, bsz, n_meta)
            ob_m = _swa_attn(z_m, n_meta, z_r, z_m, zo[4], zo[5], bias_meta, type_meta, kv_meta_q, bsz, 1, n_meta)
            hm = _merge_out_ln(oa_m, ob_m, z_m, zo[6], zo[7], lw["wa"], lw["wb"], lw["wo"], hm, ln_mix_g[l],
                               ln_mix_b[l], alpha)

        if l % 2 == 0:
            w1, w3, w2 = (ffn_w1[l // 2].astype(BF16), ffn_w3[l // 2].astype(BF16), ffn_w2[l // 2].astype(BF16))
            hr, hr_b = _ffn_ln(hr, w1, w3, w2, ln_ffn_g[l], ln_ffn_b[l], alpha)
            if not last:
                hm, hm_b = _ffn_ln(hm, w1, w3, w2, ln_ffn_g[l], ln_ffn_b[l], alpha)
        else:
            hr = _moe_layer(hr, router_w[l // 2], router_b[l // 2], moe_w1[l // 2], moe_w3[l // 2], moe_w2[l // 2],
                            ln_ffn_g[l], ln_ffn_b[l], alpha)
            hr_b = hr.astype(BF16)
            if not last:
                hm = _moe_layer(hm, router_w[l // 2], router_b[l // 2], moe_w1[l // 2], moe_w3[l // 2],
                                moe_w2[l // 2], ln_ffn_g[l], ln_ffn_b[l], alpha)
                hm_b = hm.astype(BF16)
    return hr.reshape(bsz, seq, d)


def _moe_layer(h, router_w, router_b, w1, w3, w2, lg, lb, alpha):
    m, d = h.shape
    n_exp = router_w.shape[1]
    tm = _tile(m, 512)
    w_pad = jnp.zeros((d, V7X_LANES), F32).at[:, :n_exp].set(router_w)
    b_pad = jnp.zeros((1, V7X_LANES), F32).at[0, :n_exp].set(router_b)
    pk, cnt = _router(h, w_pad, b_pad, n_exp)
    counts = cnt[0, :n_exp].astype(jnp.int32)
    padded = (counts + tm - 1) // tm * tm
    pad_end = jnp.cumsum(padded)
    pad_start = pad_end - padded
    e1 = pk[:, _PK_E1].astype(jnp.int32)
    e2 = pk[:, _PK_E2].astype(jnp.int32)
    d1 = pad_start[e1] + pk[:, _PK_R1].astype(jnp.int32)
    d2 = pad_start[e2] + pk[:, _PK_R2].astype(jnp.int32)
    n_tiles = (m * TOP_K) // tm + n_exp
    n_slots = n_tiles * tm
    n_used = pad_end[-1] // tm
    tile_id = jnp.arange(n_tiles, dtype=jnp.int32)
    tile_valid = (tile_id < n_used).astype(jnp.int32)
    tile_src = jnp.minimum(tile_id, n_used - 1).astype(jnp.int32)
    tile_expert = jnp.minimum(jnp.searchsorted(pad_end, tile_src * tm, side="right"), n_exp - 1).astype(jnp.int32)
    pad_info = jnp.stack([jnp.append(pad_start + counts, pad_end[-1]),
                          jnp.append(padded - counts, n_slots - pad_end[-1])]).astype(jnp.int32)

    xs = _dispatch(h, d1, d2, pad_info, n_slots)
    ys = _moe_ffn(xs, w1.astype(BF16), w3.astype(BF16), w2.astype(BF16), tile_expert, tile_src, tile_valid, tm)
    return _combine_ln(h, pk, ys, d1, d2, lg, lb, alpha)
```
